```python
import numpy as np
import jax, jax.numpy as jnp
from jax import lax

D_MODEL = 1024
BATCH = 32
SEQ = 2048
DEPTH = 2

HEAD_DIM = 64
MOBA_HEADS = D_MODEL // (2 * HEAD_DIM)
NSA_HEADS = D_MODEL // (2 * HEAD_DIM)
NSA_KV_GROUPS = 2
NSA_HPG = NSA_HEADS // NSA_KV_GROUPS
MOBA_BLOCK = 256
MOBA_TOPK = 3
CMP_LEN = 32
CMP_STRIDE = 16
SLC_BLOCK = 64
SLC_TOPN = 16
WINDOW = 512
Q_CHUNK = 128
RMS_EPS = 1e-6
NEG = -1e9
FORCE_BONUS = 1e4

MOBA_W = MOBA_HEADS * HEAD_DIM
NSA_W = NSA_HEADS * HEAD_DIM
KV_W = NSA_KV_GROUPS * HEAD_DIM
SPLITS = [MOBA_W] * 4 + [NSA_W] + [KV_W] * 6 + [3 * NSA_HEADS] + [NSA_W]
D_IN = sum(SPLITS)

kernel_name = "hybrid_moba_nsa_sandwich_adaln"


def rmsnorm(x, g):
    xf = x.astype(jnp.float32)
    y = xf * lax.rsqrt(jnp.mean(xf * xf, axis=-1, keepdims=True) + RMS_EPS)
    return (y * g.astype(jnp.float32)).astype(x.dtype)


def alibi_slopes(n):
    return jnp.asarray(2.0 ** (-8.0 * np.arange(1, n + 1) / n), dtype=jnp.float32)


def cmp_to_slc_matrix(n_cmp, n_slc):
    i = np.arange(n_cmp)[:, None]
    j = np.arange(n_slc)[None, :]
    start = i * CMP_STRIDE
    end = start + CMP_LEN
    ov = (start < (j + 1) * SLC_BLOCK) & (end > j * SLC_BLOCK)
    return jnp.asarray(ov.astype(np.float32))


def compress_blocks(k, pe, w1, w2):
    B, S, G, dh = k.shape
    T = (S - CMP_LEN) // CMP_STRIDE + 1
    idx = np.arange(T)[:, None] * CMP_STRIDE + np.arange(CMP_LEN)[None, :]
    win = k[:, idx] + pe[:, None, :]
    win = win.transpose(0, 3, 1, 2, 4).reshape(B, G, T, CMP_LEN * dh)
    return jax.nn.gelu(win @ w1) @ w2


def hybrid_mixer(h, w_in, w_out, pe_k, pe_v, w_ck1, w_ck2, w_cv1, w_cv2):
    B, S, _ = h.shape
    dh = HEAD_DIM
    NC = S // Q_CHUNK
    NB = -(-S // MOBA_BLOCK)
    KM = min(MOBA_TOPK, NB)
    NBS = S // SLC_BLOCK
    NS = min(SLC_TOPN, NBS)
    T = (S - CMP_LEN) // CMP_STRIDE + 1
    scale = dh ** -0.5

    u = h @ w_in
    offs = np.cumsum(SPLITS)[:-1].tolist()
    (qm, km, vm, zm, qn, kc, vc, ks, vs, kw, vw, gl, zn) = jnp.split(u, offs, axis=-1)

    def heads(a, n):
        return a.reshape(B, S, n, dh).transpose(0, 2, 1, 3)

    pad = NB * MOBA_BLOCK - S
    qm = heads(qm, MOBA_HEADS) * scale
    km = jnp.pad(heads(km, MOBA_HEADS), ((0, 0), (0, 0), (0, pad), (0, 0))).reshape(B, MOBA_HEADS, NB, MOBA_BLOCK, dh)
    vm = jnp.pad(heads(vm, MOBA_HEADS), ((0, 0), (0, 0), (0, pad), (0, 0))).reshape(B, MOBA_HEADS, NB, MOBA_BLOCK, dh)
    kmean = jnp.mean(km, axis=3)

    G = NSA_KV_GROUPS
    qn = qn.reshape(B, S, G, NSA_HPG, dh).transpose(0, 2, 3, 1, 4) * scale
    kc = compress_blocks(kc.reshape(B, S, G, dh), pe_k, w_ck1, w_ck2)
    vc = compress_blocks(vc.reshape(B, S, G, dh), pe_v, w_cv1, w_cv2)
    ks = heads(ks, G).reshape(B, G, NBS, SLC_BLOCK, dh)
    vs = heads(vs, G).reshape(B, G, NBS, SLC_BLOCK, dh)
    kw = jnp.pad(heads(kw, G), ((0, 0), (0, 0), (WINDOW, 0), (0, 0)))
    vw = jnp.pad(heads(vw, G), ((0, 0), (0, 0), (WINDOW, 0), (0, 0)))
    gates = jax.nn.sigmoid(gl.astype(jnp.float32)).reshape(B, S, G, NSA_HPG, 3)

    slope_m = alibi_slopes(MOBA_HEADS)
    slope_n = alibi_slopes(NSA_HEADS).reshape(G, NSA_HPG)
    M = cmp_to_slc_matrix(T, NBS)
    hidx = jnp.arange(MOBA_HEADS)[:, None, None]
    gidx = jnp.arange(G)[:, None, None]

    def chunk(i):
        b = i // NC
        s0 = (i % NC) * Q_CHUNK
        t = s0 + jnp.arange(Q_CHUNK)
        take = lambda a: lax.dynamic_index_in_dim(a, b, 0, keepdims=False)

        q = lax.dynamic_slice_in_dim(take(qm), s0, Q_CHUNK, axis=1)
        Kb, Vb, Kmean = take(km), take(vm), take(kmean)
        cur = s0 // MOBA_BLOCK
        gsc = jnp.einsum('hqd,hnd->hqn', q, Kmean).astype(jnp.float32)
        gsc = jnp.where(jnp.arange(NB) < cur, gsc, NEG)
        _, sel = lax.top_k(gsc, KM)
        sel_ok = jnp.arange(KM) < cur
        k_sel = Kb[hidx, sel]
        v_sel = Vb[hidx, sel]
        s_sel = jnp.einsum('hqd,hqkbd->hqkb', q, k_sel).astype(jnp.float32)
        pos_sel = sel[..., None] * MOBA_BLOCK + jnp.arange(MOBA_BLOCK)
        d_sel = (t[None, :, None, None] - pos_sel).astype(jnp.float32)
        s_sel = jnp.where(sel_ok[None, None, :, None], s_sel - slope_m[:, None, None, None] * d_sel, NEG)
        k_own = lax.dynamic_index_in_dim(Kb, cur, 1, keepdims=False)
        v_own = lax.dynamic_index_in_dim(Vb, cur, 1, keepdims=False)
        s_own = jnp.einsum('hqd,hkd->hqk', q, k_own).astype(jnp.float32)
        d_own = t[:, None] - (cur * MOBA_BLOCK + jnp.arange(MOBA_BLOCK))[None, :]
        s_own = jnp.where(d_own >= 0, s_own - slope_m[:, None, None] * d_own.astype(jnp.float32), NEG)
        logits = jnp.concatenate([s_sel.reshape(MOBA_HEADS, Q_CHUNK, KM * MOBA_BLOCK), s_own], axis=-1)
        p = jax.nn.softmax(logits, axis=-1).astype(v_own.dtype)
        p_sel = p[..., :KM * MOBA_BLOCK].reshape(MOBA_HEADS, Q_CHUNK, KM, MOBA_BLOCK)
        o_m = (jnp.einsum('hqkb,hqkbd->hqd', p_sel, v_sel)
               + jnp.einsum('hqb,hbd->hqd', p[..., KM * MOBA_BLOCK:], v_own))
        o_m = o_m.transpose(1, 0, 2).reshape(Q_CHUNK, MOBA_W)

        qg = lax.dynamic_slice_in_dim(take(qn), s0, Q_CHUNK, axis=2)
        kcb, vcb = take(kc), take(vc)
        s_c = jnp.einsum('gjqd,gtd->gjqt', qg, kcb).astype(jnp.float32)
        d_c = t[:, None] - (jnp.arange(T) * CMP_STRIDE + CMP_LEN - 1)[None, :]
        ok_c = d_c >= 0
        s_c = jnp.where(ok_c, s_c - slope_n[:, :, None, None] * d_c.astype(jnp.float32), NEG)
        p_c = jax.nn.softmax(s_c, axis=-1) * ok_c
        o_c = jnp.einsum('gjqt,gtd->gjqd', p_c.astype(vcb.dtype), vcb)
        imp = jnp.einsum('gjqt,tn->gqn', p_c, M)
        blk = t // SLC_BLOCK
        jn = jnp.arange(NBS)[None, :]
        forced = (jn == 0) | (jn == blk[:, None]) | (jn == blk[:, None] - 1)
        imp = jnp.where(forced, FORCE_BONUS, imp)
        imp = jnp.where(jn <= blk[:, None], imp, NEG)
        _, sidx = lax.top_k(imp, NS)
        s_valid = sidx <= blk[None, :, None]
        ksb = take(ks)[gidx, sidx]
        vsb = take(vs)[gidx, sidx]
        s_s = jnp.einsum('gjqd,gqnbd->gjqnb', qg, ksb).astype(jnp.float32)
        d_s = t[None, :, None, None] - (sidx[..., None] * SLC_BLOCK + jnp.arange(SLC_BLOCK))
        ok_s = (d_s >= 0) & s_valid[..., None]
        s_s = jnp.where(ok_s[:, None], s_s - slope_n[:, :, None, None, None] * d_s[:, None].astype(jnp.float32), NEG)
        p_s = jax.nn.softmax(s_s.reshape(G, NSA_HPG, Q_CHUNK, NS * SLC_BLOCK), axis=-1)
        p_s = p_s.reshape(G, NSA_HPG, Q_CHUNK, NS, SLC_BLOCK).astype(vsb.dtype)
        o_s = jnp.einsum('gjqnb,gqnbd->gjqd', p_s, vsb)
        kwb = lax.dynamic_slice_in_dim(take(kw), s0, WINDOW + Q_CHUNK, axis=1)
        vwb = lax.dynamic_slice_in_dim(take(vw), s0, WINDOW + Q_CHUNK, axis=1)
        s_w = jnp.einsum('gjqd,gkd->gjqk', qg, kwb).astype(jnp.float32)
        pos_w = s0 - WINDOW + jnp.arange(WINDOW + Q_CHUNK)
        d_w = t[:, None] - pos_w[None, :]
        ok_w = (d_w >= 0) & (d_w < WINDOW) & (pos_w >= 0)[None, :]
        s_w = jnp.where(ok_w, s_w - slope_n[:, :, None, None] * d_w.astype(jnp.float32), NEG)
        p_w = jax.nn.softmax(s_w, axis=-1).astype(vwb.dtype)
        o_w = jnp.einsum('gjqk,gkd->gjqd', p_w, vwb)
        g = lax.dynamic_slice_in_dim(take(gates), s0, Q_CHUNK, axis=0).transpose(1, 2, 0, 3)
        g = g.astype(o_c.dtype)
        o_n = g[..., 0:1] * o_c + g[..., 1:2] * o_s + g[..., 2:3] * o_w
        o_n = o_n.transpose(2, 0, 1, 3).reshape(Q_CHUNK, NSA_W)
        return o_m, o_n

    o_m, o_n = lax.map(chunk, jnp.arange(B * NC))
    o_m = o_m.reshape(B, S, MOBA_W)
    o_n = o_n.reshape(B, S, NSA_W)
    y = jnp.concatenate([o_m * jax.nn.silu(zm), o_n * jax.nn.silu(zn)], axis=-1)
    return y @ w_out


def setup_inputs(seed: int = 0) -> dict:
    key = jax.random.key(seed)
    ks = jax.random.split(key, 16)
    L, D, dh = DEPTH, D_MODEL, HEAD_DIM
    nrm = lambda k, shape, s: jax.random.normal(k, shape, jnp.float32) * s
    return {
        "x": nrm(ks[0], (BATCH, SEQ, D), 1.0),
        "c": nrm(ks[1], (BATCH, D), 1.0),
        "w_ada": nrm(ks[2], (L, D, 3 * D), D ** -0.5),
        "b_ada": nrm(ks[3], (L, 3 * D), 0.01),
        "g_pre": 1.0 + nrm(ks[4], (L, D), 0.1),
        "g_post": 1.0 + nrm(ks[5], (L, D), 0.1),
        "w_in": nrm(ks[6], (L, D, D_IN), D ** -0.5),
        "w_out": nrm(ks[7], (L, D, D), D ** -0.5),
        "pe_k": nrm(ks[8], (L, CMP_LEN, dh), 0.1),
        "pe_v": nrm(ks[9], (L, CMP_LEN, dh), 0.1),
        "w_ck1": nrm(ks[10], (L, CMP_LEN * dh, dh), (CMP_LEN * dh) ** -0.5),
        "w_ck2": nrm(ks[11], (L, dh, dh), dh ** -0.5),
        "w_cv1": nrm(ks[12], (L, CMP_LEN * dh, dh), (CMP_LEN * dh) ** -0.5),
        "w_cv2": nrm(ks[13], (L, dh, dh), dh ** -0.5),
    }


def reference(x, c, w_ada, b_ada, g_pre, g_post, w_in, w_out, pe_k, pe_v, w_ck1, w_ck2, w_cv1, w_cv2):
    D = x.shape[-1]
    cs = jax.nn.silu(c)
    for l in range(DEPTH):
        mod = cs @ w_ada[l] + b_ada[l]
        shift, scale, gate = mod[:, None, :D], mod[:, None, D:2 * D], mod[:, None, 2 * D:]
        h = rmsnorm(x, g_pre[l]) * (1.0 + scale) + shift
        y = hybrid_mixer(h, w_in[l], w_out[l], pe_k[l], pe_v[l], w_ck1[l], w_ck2[l], w_cv1[l], w_cv2[l])
        x = x + gate * rmsnorm(y, g_post[l])
    return x
```

```python
import functools

import numpy as np
import jax
import jax.numpy as jnp
from jax import lax
from jax.experimental import pallas as pl
from jax.experimental.pallas import tpu as pltpu

HEAD_DIM = 64
MOBA_HEADS = 8
NSA_HEADS = 8
NSA_KV_GROUPS = 2
NSA_HPG = NSA_HEADS // NSA_KV_GROUPS
MOBA_BLOCK = 256
MOBA_TOPK = 3
CMP_LEN = 32
CMP_STRIDE = 16
SLC_BLOCK = 64
SLC_TOPN = 16
WINDOW = 512
RMS_EPS = 1e-6
NEG = -1e9
FORCE_BONUS = 1e4

LANES = 128
TILE = 256
POS_RADIX = 16
ROW_TILE = 512
VMEM_LIMIT = 56 * 1024 * 1024

MOBA_SEL0 = HEAD_DIM
MOBA_POS0 = MOBA_SEL0 + 8
NSA_SEL0 = HEAD_DIM
NSA_POS0 = NSA_SEL0 + 32

BF16 = jnp.bfloat16
F32 = jnp.float32


def _slope(i, n):
    return 2.0 ** (-8.0 * (i + 1) / n)


def _dot(a, b):
    return jnp.dot(a, b, preferred_element_type=F32)


def _dot_nt(a, b):
    return lax.dot_general(a, b, (((1,), (1,)), ((), ())), preferred_element_type=F32)


def _sigmoid(x):
    return 1.0 / (1.0 + jnp.exp(-x))


def _cparams(n_axes):
    return pltpu.CompilerParams(
        dimension_semantics=("arbitrary",) * n_axes, vmem_limit_bytes=VMEM_LIMIT)


def _mod_kernel(c_ref, w_ref, b_ref, o_ref):
    c = c_ref[...]
    cs = (c * _sigmoid(c)).astype(BF16)
    o_ref[0] = _dot(cs, w_ref[0].astype(BF16)) + b_ref[0]


def _modulation(c, w_ada, b_ada):
    L, D, N = w_ada.shape
    B = c.shape[0]
    tn = 512
    return pl.pallas_call(
        _mod_kernel,
        grid=(L, N // tn),
        in_specs=[
            pl.BlockSpec((B, D), lambda l, j: (0, 0)),
            pl.BlockSpec((1, D, tn), lambda l, j: (l, 0, j)),
            pl.BlockSpec((1, 1, tn), lambda l, j: (l, 0, j)),
        ],
        out_specs=pl.BlockSpec((1, B, tn), lambda l, j: (l, 0, j)),
        out_shape=jax.ShapeDtypeStruct((L, B, N), F32),
        compiler_params=_cparams(2),
        name="adaln_mod",
    )(c, w_ada, b_ada.reshape(L, 1, N))


_SEG = {}
_off = 0
for _name, _w in (("qm", 512), ("km", 512), ("vm", 512), ("zm", 512), ("qn", 512),
                  ("kc", 128), ("vc", 128), ("ks", 128), ("vs", 128), ("kw", 128),
                  ("vw", 128), ("zn", 512), ("gl", 128)):
    _SEG[_name] = (_off, _w)
    _off += _w
D_IN_PACKED = _off


def _pack_w_in(w_in):
    offs = np.cumsum([0, 512, 512, 512, 512, 512, 128, 128, 128, 128, 128, 128, 24, 512])
    parts = [w_in[:, offs[i]:offs[i + 1]] for i in range(13)]
    gl, zn = parts[11], parts[12]
    gl = jnp.pad(gl, ((0, 0), (0, LANES - gl.shape[1])))
    return jnp.concatenate(parts[:11] + [zn, gl], axis=1).astype(BF16)


def _inproj_kernel(x_ref, mod_ref, g_ref, w_ref,
                   qm_ref, km_ref, vm_ref, kmean_ref, gm_ref,
                   qn_ref, ks_ref, vs_ref, kw_ref, vw_ref,
                   kc_ref, vc_ref, gate_ref, gn_ref, *, seq_len):
    tm, D = x_ref.shape
    x = x_ref[...]
    mod = mod_ref[0]
    shift, scale = mod[:, :D], mod[:, D:2 * D]
    y = x * lax.rsqrt(jnp.mean(x * x, axis=-1, keepdims=True) + RMS_EPS) * g_ref[...]
    hb = (y * (1.0 + scale) + shift).astype(BF16)

    def seg(name):
        o, w = _SEG[name]
        return _dot(hb, w_ref[:, o:o + w])

    lane = lax.broadcasted_iota(jnp.int32, (tm, LANES), 1)
    row = lax.broadcasted_iota(jnp.int32, (tm, LANES), 0)
    t = (pl.program_id(0) % (seq_len // tm)) * tm + row
    t_hi = (t // POS_RADIX).astype(F32)
    t_lo = (t % POS_RADIX).astype(F32)

    def pos_lanes(p0, v0, v1, v2, v3):
        z = jnp.zeros((tm, LANES), F32)
        return jnp.where(lane == p0, v0, jnp.where(lane == p0 + 1, v1,
                         jnp.where(lane == p0 + 2, v2, jnp.where(lane == p0 + 3, v3, z))))

    one = jnp.ones((tm, LANES), F32)
    q_pos_m = pos_lanes(MOBA_POS0, POS_RADIX * one, one, -POS_RADIX * t_hi, -t_lo)
    k_pos_m = pos_lanes(MOBA_POS0, t_hi, t_lo, one, one)
    q_pos_n = pos_lanes(NSA_POS0, POS_RADIX * one, one, -POS_RADIX * t_hi, -t_lo)
    k_pos_n = pos_lanes(NSA_POS0, t_hi, t_lo, one, one)
    k_ext_m = k_pos_m + jnp.where(lane - MOBA_SEL0 == t // MOBA_BLOCK, 1.0, 0.0)
    k_ext_s = k_pos_n + jnp.where(lane - NSA_SEL0 == t // SLC_BLOCK, 1.0, 0.0)
    v_ext = jnp.where(lane == HEAD_DIM, 1.0, 0.0)
    low = lane < HEAD_DIM

    def head(u, h, ext):
        p = u[:, (h // 2) * LANES:(h // 2 + 1) * LANES]
        if h % 2:
            p = pltpu.roll(p, HEAD_DIM, axis=1)
        return jnp.where(low, p, ext).astype(BF16)

    u = seg("qm") * (HEAD_DIM ** -0.5)
    for h in range(MOBA_HEADS):
        qm_ref[0, h] = head(u, h, _slope(h, MOBA_HEADS) * q_pos_m)
    u = seg("km")
    for h in range(MOBA_HEADS):
        km_ref[0, h] = head(u, h, k_ext_m)
    for r in range(tm // MOBA_BLOCK):
        kmean_ref[0, r] = jnp.mean(u[r * MOBA_BLOCK:(r + 1) * MOBA_BLOCK], axis=0, keepdims=True)
    u = seg("vm")
    for h in range(MOBA_HEADS):
        vm_ref[0, h] = head(u, h, v_ext)
    u = seg("zm")
    gm_ref[...] = (u * _sigmoid(u)).astype(BF16)
    u = seg("qn") * (HEAD_DIM ** -0.5)
    for h in range(NSA_HEADS):
        qn_ref[0, h] = head(u, h, _slope(h, NSA_HEADS) * q_pos_n)
    for name, ref, ext in (("ks", ks_ref, k_ext_s), ("vs", vs_ref, v_ext),
                           ("kw", kw_ref, k_pos_n), ("vw", vw_ref, v_ext)):
        u = seg(name)
        for g in range(NSA_KV_GROUPS):
            ref[0, g] = head(u, g, ext)
    kc_ref[...] = seg("kc")
    vc_ref[...] = seg("vc")
    u = seg("zn")
    gn_ref[...] = (u * _sigmoid(u)).astype(BF16)
    sg = _sigmoid(seg("gl"))
    for g in range(NSA_KV_GROUPS):
        gate_ref[0, g] = sg if g == 0 else pltpu.roll(sg, LANES - g * 3 * NSA_HPG, axis=1)


def _inproj(x2, mod, g_pre, w_packed, B, S):
    M, D = x2.shape
    tm = ROW_TILE
    nt = S // tm
    grid = (M // tm,)
    bs = lambda i: i // nt
    si = lambda i: i % nt
    head_spec = lambda nh: pl.BlockSpec((1, nh, tm, LANES), lambda i: (bs(i), 0, si(i), 0))
    row_spec = lambda w: pl.BlockSpec((tm, w), lambda i: (i, 0))
    sds = jax.ShapeDtypeStruct
    out_shape = (
        sds((B, MOBA_HEADS, S, LANES), BF16),
        sds((B, MOBA_HEADS, S, LANES), BF16),
        sds((B, MOBA_HEADS, S, LANES), BF16),
        sds((B, S // MOBA_BLOCK, 1, 512), F32),
        sds((M, 512), BF16),
        sds((B, NSA_HEADS, S, LANES), BF16),
        sds((B, NSA_KV_GROUPS, S, LANES), BF16),
        sds((B, NSA_KV_GROUPS, S, LANES), BF16),
        sds((B, NSA_KV_GROUPS, S, LANES), BF16),
        sds((B, NSA_KV_GROUPS, S, LANES), BF16),
        sds((M, LANES), F32),
        sds((M, LANES), F32),
        sds((B, NSA_KV_GROUPS, S, LANES), F32),
        sds((M, 512), BF16),
    )
    out_specs = (
        head_spec(MOBA_HEADS), head_spec(MOBA_HEADS), head_spec(MOBA_HEADS),
        pl.BlockSpec((1, tm // MOBA_BLOCK, 1, 512), lambda i: (bs(i), si(i), 0, 0)),
        row_spec(512),
        head_spec(NSA_HEADS),
        head_spec(NSA_KV_GROUPS), head_spec(NSA_KV_GROUPS),
        head_spec(NSA_KV_GROUPS), head_spec(NSA_KV_GROUPS),
        row_spec(LANES), row_spec(LANES),
        head_spec(NSA_KV_GROUPS),
        row_spec(512),
    )
    return pl.pallas_call(
        functools.partial(_inproj_kernel, seq_len=S),
        grid=grid,
        in_specs=[
            row_spec(D),
            pl.BlockSpec((1, 1, 3 * D), lambda i: (bs(i), 0, 0)),
            pl.BlockSpec((1, D), lambda i: (0, 0)),
            pl.BlockSpec((D, D_IN_PACKED), lambda i: (0, 0)),
        ],
        out_specs=out_specs,
        out_shape=out_shape,
        compiler_params=_cparams(1),
        name="in_proj",
    )(x2, mod, g_pre, w_packed)


def _gelu_tanh(x):
    return 0.5 * x * (1.0 + jnp.tanh(np.sqrt(2.0 / np.pi) * (x + 0.044715 * (x * x * x))))


def _compress_kernel(kc_ref, vc_ref, pek_ref, pev_ref, wk1_ref, wk2_ref, wv1_ref, wv2_ref,
                     m_ref, kout_ref, vout_ref, win_ref):
    S = kc_ref.shape[0]
    nrow = S // CMP_STRIDE
    lane = lax.broadcasted_iota(jnp.int32, (nrow, LANES), 1)
    row = lax.broadcasted_iota(jnp.int32, (nrow, LANES), 0)
    low = lane < HEAD_DIM
    valid = row < nrow - 1
    k_ext = jnp.where(lane == NSA_POS0, (row + 1).astype(F32),
                      jnp.where(lane == NSA_POS0 + 1, float(POS_RADIX - 1),
                                jnp.where((lane == NSA_POS0 + 2) | (lane == NSA_POS0 + 3), 1.0, 0.0)))

    def run(src_ref, pe_ref, w1_ref, w2_ref, out_ref, ext):
        for l in range(CMP_STRIDE):
            win_ref[:, l * LANES:(l + 1) * LANES] = src_ref[pl.ds(l, nrow, stride=CMP_STRIDE), :]
        xw = win_ref[...]
        lo = _dot((xw + pe_ref[0:1, :]).astype(BF16), w1_ref[0])
        hi = _dot((xw + pe_ref[1:2, :]).astype(BF16), w1_ref[1])
        pre = lo + pltpu.roll(hi, nrow - 1, axis=0)
        out = _dot(_gelu_tanh(pre).astype(BF16), w2_ref[...])
        out = jnp.where(valid, out, 0.0)
        for g in range(NSA_KV_GROUPS):
            p = out if g == 0 else pltpu.roll(out, HEAD_DIM, axis=1)
            out_ref[0, g] = jnp.where(low, p, ext).astype(BF16)

    run(kc_ref, pek_ref, wk1_ref, wk2_ref, kout_ref, k_ext)
    run(vc_ref, pev_ref, wv1_ref, wv2_ref, vout_ref, jnp.where(valid, m_ref[...], 0.0))


def _pack_compress_weights(pe, w1, w2):
    half = CMP_STRIDE
    eye = jnp.eye(NSA_KV_GROUPS, dtype=F32)
    w1r = w1.reshape(2, half, HEAD_DIM, HEAD_DIM)
    w1bd = jnp.einsum("hlde,gk->hlgdke", w1r, eye).reshape(2, half * LANES, LANES).astype(BF16)
    w2bd = jnp.einsum("de,gk->gdke", w2, eye).reshape(LANES, LANES).astype(BF16)
    pet = jnp.tile(pe.reshape(2, half, 1, HEAD_DIM), (1, 1, NSA_KV_GROUPS, 1)).reshape(2, half * LANES)
    return pet, w1bd, w2bd


def _cmp_to_slc_lanes(n_rows, n_slc):
    i = np.arange(n_rows)[:, None]
    j = np.arange(n_slc)[None, :]
    start = i * CMP_STRIDE
    ov = (start < (j + 1) * SLC_BLOCK) & (start + CMP_LEN > j * SLC_BLOCK)
    m = np.zeros((n_rows, LANES), np.float32)
    m[:, NSA_SEL0:NSA_SEL0 + n_slc] = ov
    return jnp.asarray(m)


def _compress(kc, vc, pk, pv, B, S):
    nrow = S // CMP_STRIDE
    pek, wk1, wk2 = pk
    pev, wv1, wv2 = pv
    m = _cmp_to_slc_lanes(nrow, S // SLC_BLOCK)
    full = lambda a: pl.BlockSpec(a.shape, lambda b: (0,) * a.ndim)
    out_spec = pl.BlockSpec((1, NSA_KV_GROUPS, nrow, LANES), lambda b: (b, 0, 0, 0))
    out_sds = jax.ShapeDtypeStruct((B, NSA_KV_GROUPS, nrow, LANES), BF16)
    return pl.pallas_call(
        _compress_kernel,
        grid=(B,),
        in_specs=[pl.BlockSpec((S, LANES), lambda b: (b, 0)), pl.BlockSpec((S, LANES), lambda b: (b, 0)),
                  full(pek), full(pev), full(wk1), full(wk2), full(wv1), full(wv2), full(m)],
        out_specs=(out_spec, out_spec),
        out_shape=(out_sds, out_sds),
        scratch_shapes=[pltpu.VMEM((nrow, CMP_STRIDE * LANES), F32)],
        compiler_params=_cparams(1),
        name="nsa_compress",
    )(kc, vc, pek, pev, wk1, wk2, wv1, wv2, m)


def _rank_desc(vals, lane0, count):
    idx = lax.broadcasted_iota(jnp.int32, vals.shape, 1) - lane0
    rank = jnp.zeros(vals.shape, jnp.int32)
    for j in range(count):
        col = vals[:, lane0 + j:lane0 + j + 1]
        beats = (col > vals) | ((col == vals) & (j < idx))
        rank = rank + beats.astype(jnp.int32)
    return rank


def _tile_masks():
    r = lax.broadcasted_iota(jnp.int32, (TILE, TILE), 0)
    c = lax.broadcasted_iota(jnp.int32, (TILE, TILE), 1)
    return c <= r, c > r


def _first_tile(q, k, v, mask):
    s = _dot_nt(q, k)
    s = jnp.where(mask, s, NEG)
    m = jnp.max(s, axis=1, keepdims=True)
    p = jnp.exp(s - m)
    return m, _dot(p.astype(BF16), v)


def _next_tile(carry, q, k, v, mask=None):
    m, acc = carry
    s = _dot_nt(q, k)
    if mask is not None:
        s = jnp.where(mask, s, NEG)
    m_new = jnp.maximum(m, jnp.max(s, axis=1, keepdims=True))
    p = jnp.exp(s - m_new)
    return m_new, jnp.exp(m - m_new) * acc + _dot(p.astype(BF16), v)


def _causal_sweep(q, k_ref, v_ref, qi, causal):
    def kv(j):
        sl = pl.ds(pl.multiple_of(j * TILE, TILE), TILE)
        return k_ref[sl, :], v_ref[sl, :]

    carry = _first_tile(q, *kv(qi), causal)
    carry = lax.fori_loop(0, qi, lambda j, c: _next_tile(c, q, *kv(j)), carry)
    return carry[1]


def _normalize(acc):
    return acc / acc[:, HEAD_DIM:HEAD_DIM + 1]


def _pair(a, b):
    lane = lax.broadcasted_iota(jnp.int32, a.shape, 1)
    return jnp.where(lane < HEAD_DIM, a, pltpu.roll(b, HEAD_DIM, axis=1))


def _moba_kernel(q_ref, k_ref, v_ref, kmean_ref, gate_ref, o_ref):
    qi = pl.program_id(2)
    lane = lax.broadcasted_iota(jnp.int32, (TILE, LANES), 1)
    blk = lane - MOBA_SEL0
    in_sel = (blk >= 0) & (lane < MOBA_POS0)
    causal, _ = _tile_masks()
    outs = []
    for h in range(2):
        q = q_ref[0, h]
        gsc = jnp.where(in_sel & (blk < qi), _dot_nt(q, kmean_ref[0, h]), NEG)
        rank = _rank_desc(gsc, MOBA_SEL0, 8)
        keep = ((blk < qi) & (rank < jnp.minimum(MOBA_TOPK, qi))) | (blk == qi)
        bias = jnp.where(in_sel & jnp.logical_not(keep), NEG, 0.0).astype(BF16)
        qb = jnp.where(in_sel, bias, q)
        outs.append(_normalize(_causal_sweep(qb, k_ref.at[0, h], v_ref.at[0, h], qi, causal)))
    o_ref[...] = (_pair(*outs) * gate_ref[...].astype(F32)).astype(BF16)


def _moba(qm, km, vm, kmean, gm, B, S):
    nq = S // TILE
    return pl.pallas_call(
        _moba_kernel,
        grid=(B, MOBA_HEADS // 2, nq),
        in_specs=[
            pl.BlockSpec((1, 2, TILE, LANES), lambda b, p, i: (b, p, i, 0)),
            pl.BlockSpec((1, 2, S, LANES), lambda b, p, i: (b, p, 0, 0)),
            pl.BlockSpec((1, 2, S, LANES), lambda b, p, i: (b, p, 0, 0)),
            pl.BlockSpec((1, 2, LANES, LANES), lambda b, p, i: (b, p, 0, 0)),
            pl.BlockSpec((TILE, LANES), lambda b, p, i: (b * nq + i, p)),
        ],
        out_specs=pl.BlockSpec((TILE, LANES), lambda b, p, i: (b * nq + i, p)),
        out_shape=jax.ShapeDtypeStruct((B * S, MOBA_HEADS * HEAD_DIM), BF16),
        compiler_params=_cparams(3),
        name="moba_attn",
    )(qm, km, vm, kmean, gm)


def _nsa_kernel(q_ref, ks_ref, vs_ref, kw_ref, vw_ref, kc_ref, vc_ref, bg_ref, gate_ref, o_ref):
    qi = pl.program_id(2)
    lane = lax.broadcasted_iota(jnp.int32, (TILE, LANES), 1)
    row = lax.broadcasted_iota(jnp.int32, (TILE, LANES), 0)
    t = qi * TILE + row
    causal, anti = _tile_masks()

    ok_c = t >= lane * CMP_STRIDE + (CMP_LEN - 1)
    kc, vc = kc_ref[0, 0], vc_ref[0, 0]
    o_cmp = []
    for j in range(NSA_HPG):
        s = jnp.where(ok_c, _dot_nt(q_ref[0, j], kc), NEG)
        e = jnp.exp(s - jnp.max(s, axis=1, keepdims=True))
        p = jnp.where(ok_c, e / jnp.sum(e, axis=1, keepdims=True), 0.0)
        o_cmp.append(_dot(p.astype(BF16), vc))
    imp = (o_cmp[0] + o_cmp[1]) + (o_cmp[2] + o_cmp[3])

    n = lane - NSA_SEL0
    in_sel = (n >= 0) & (lane < NSA_POS0)
    blk = t // SLC_BLOCK
    forced = (n == 0) | (n == blk) | (n == blk - 1)
    imp = jnp.where(forced, FORCE_BONUS, imp)
    imp = jnp.where(in_sel & (n <= blk), imp, NEG)
    rank = _rank_desc(imp, NSA_SEL0, 32)
    keep = (rank < SLC_TOPN) & (n <= blk)
    bias = jnp.where(in_sel & jnp.logical_not(keep), NEG, 0.0).astype(BF16)

    ks, vs, kw, vw = ks_ref.at[0, 0], vs_ref.at[0, 0], kw_ref.at[0, 0], vw_ref.at[0, 0]
    bg = bg_ref[0, 0]
    outs = []
    for j in range(NSA_HPG):
        q = q_ref[0, j]
        o_sel = _normalize(_causal_sweep(jnp.where(in_sel, bias, q), ks, vs, qi, causal))

        def kv(jt):
            sl = pl.ds(pl.multiple_of(jt * TILE, TILE), TILE)
            return kw[sl, :], vw[sl, :]

        carry = _first_tile(q, *kv(qi), causal)
        carry = lax.cond(qi >= 1, lambda c: _next_tile(c, q, *kv(qi - 1)), lambda c: c, carry)
        carry = lax.cond(qi >= 2, lambda c: _next_tile(c, q, *kv(qi - 2), anti), lambda c: c, carry)
        o_win = _normalize(carry[1])

        outs.append(bg[:, 3 * j:3 * j + 1] * o_cmp[j] + bg[:, 3 * j + 1:3 * j + 2] * o_sel
                    + bg[:, 3 * j + 2:3 * j + 3] * o_win)
    o = jnp.concatenate([_pair(outs[0], outs[1]), _pair(outs[2], outs[3])], axis=1)
    o_ref[...] = (o * gate_ref[...].astype(F32)).astype(BF16)


def _nsa(qn, ks, vs, kw, vw, kcmp, vcmp, gates, gn, B, S):
    nq = S // TILE
    nrow = S // CMP_STRIDE
    kv_spec = pl.BlockSpec((1, 1, S, LANES), lambda b, g, i: (b, g, 0, 0))
    cmp_spec = pl.BlockSpec((1, 1, nrow, LANES), lambda b, g, i: (b, g, 0, 0))
    wide = NSA_HPG * HEAD_DIM
    return pl.pallas_call(
        _nsa_kernel,
        grid=(B, NSA_KV_GROUPS, nq),
        in_specs=[
            pl.BlockSpec((1, NSA_HPG, TILE, LANES), lambda b, g, i: (b, g, i, 0)),
            kv_spec, kv_spec, kv_spec, kv_spec, cmp_spec, cmp_spec,
            pl.BlockSpec((1, 1, TILE, LANES), lambda b, g, i: (b, g, i, 0)),
            pl.BlockSpec((TILE, wide), lambda b, g, i: (b * nq + i, g)),
        ],
        out_specs=pl.BlockSpec((TILE, wide), lambda b, g, i: (b * nq + i, g)),
        out_shape=jax.ShapeDtypeStruct((B * S, NSA_HEADS * HEAD_DIM), BF16),
        compiler_params=_cparams(3),
        name="nsa_attn",
    )(qn, ks, vs, kw, vw, kcmp, vcmp, gates, gn)


def _outproj_kernel(ym_ref, yn_ref, w_ref, x_ref, mod_ref, g_ref, o_ref):
    D = x_ref.shape[1]
    half = ym_ref.shape[1]
    z = _dot(ym_ref[...], w_ref[:half, :]) + _dot(yn_ref[...], w_ref[half:, :])
    r = z * lax.rsqrt(jnp.mean(z * z, axis=-1, keepdims=True) + RMS_EPS) * g_ref[...]
    o_ref[...] = x_ref[...] + mod_ref[0][:, 2 * D:] * r


def _outproj(ym, yn, w_out, x2, mod, g_post, S):
    M, D = x2.shape
    tm = ROW_TILE
    nt = S // tm
    row_spec = lambda w: pl.BlockSpec((tm, w), lambda i: (i, 0))
    return pl.pallas_call(
        _outproj_kernel,
        grid=(M // tm,),
        in_specs=[
            row_spec(ym.shape[1]), row_spec(yn.shape[1]),
            pl.BlockSpec(w_out.shape, lambda i: (0, 0)),
            row_spec(D),
            pl.BlockSpec((1, 1, 3 * D), lambda i: (i // nt, 0, 0)),
            pl.BlockSpec((1, D), lambda i: (0, 0)),
        ],
        out_specs=row_spec(D),
        out_shape=jax.ShapeDtypeStruct((M, D), F32),
        compiler_params=_cparams(1),
        name="out_proj",
    )(ym, yn, w_out, x2, mod, g_post)


def kernel(x, c, w_ada, b_ada, g_pre, g_post, w_in, w_out, pe_k, pe_v, w_ck1, w_ck2, w_cv1, w_cv2):
    B, S, D = x.shape
    L = w_ada.shape[0]
    assert D == (MOBA_HEADS + NSA_HEADS) * HEAD_DIM and S % ROW_TILE == 0
    assert S // MOBA_BLOCK <= 8 and S // SLC_BLOCK <= 32 and S // CMP_STRIDE <= LANES
    nb = S // MOBA_BLOCK

    mod_all = _modulation(c, w_ada, b_ada)
    x2 = x.reshape(B * S, D)
    for l in range(L):
        mod = mod_all[l].reshape(B, 1, 3 * D)
        (qm, km, vm, kmean, gm, qn, ks, vs, kw, vw, kc, vc, gates, gn) = _inproj(
            x2, mod, g_pre[l].reshape(1, D), _pack_w_in(w_in[l]), B, S)
        kmean = kmean.reshape(B, nb, MOBA_HEADS, HEAD_DIM).transpose(0, 2, 1, 3)
        kmean = jnp.pad(kmean, ((0, 0), (0, 0), (MOBA_SEL0, LANES - MOBA_SEL0 - nb),
                                (0, LANES - HEAD_DIM))).astype(BF16)
        kcmp, vcmp = _compress(kc, vc, _pack_compress_weights(pe_k[l], w_ck1[l], w_ck2[l]),
                               _pack_compress_weights(pe_v[l], w_cv1[l], w_cv2[l]), B, S)
        ym = _moba(qm, km, vm, kmean, gm, B, S)
        yn = _nsa(qn, ks, vs, kw, vw, kcmp, vcmp, gates, gn, B, S)
        x2 = _outproj(ym, yn, w_out[l].astype(BF16), x2, mod, g_post[l].reshape(1, D), S)
    return x2.reshape(B, S, D)
```

```python
import functools

import numpy as np
import jax
import jax.numpy as jnp
from jax import lax
from jax.experimental import pallas as pl
from jax.experimental.pallas import tpu as pltpu

HEAD_DIM = 64
MOBA_HEADS = 8
NSA_HEADS = 8
NSA_KV_GROUPS = 2
NSA_HPG = NSA_HEADS // NSA_KV_GROUPS
MOBA_BLOCK = 256
MOBA_TOPK = 3
CMP_LEN = 32
CMP_STRIDE = 16
SLC_BLOCK = 64
SLC_TOPN = 16
WINDOW = 512
RMS_EPS = 1e-6
NEG = -1e9
FORCE_BONUS = 1e4

LANES = 128
TILE = 256
POS_RADIX = 16
ROW_TILE = 512
VMEM_LIMIT = 56 * 1024 * 1024

MOBA_SEL0 = HEAD_DIM
MOBA_POS0 = MOBA_SEL0 + 8
NSA_SEL0 = HEAD_DIM
NSA_POS0 = NSA_SEL0 + 32

BF16 = jnp.bfloat16
F32 = jnp.float32


def _slope(i, n):
    return 2.0 ** (-8.0 * (i + 1) / n)


def _dot(a, b):
    return jnp.dot(a, b, preferred_element_type=F32)


def _sigmoid(x):
    return 1.0 / (1.0 + jnp.exp(-x))


def _cparams(n_axes):
    return pltpu.CompilerParams(
        dimension_semantics=("arbitrary",) * n_axes, vmem_limit_bytes=VMEM_LIMIT)


def _mod_kernel(c_ref, w_ref, b_ref, o_ref):
    c = c_ref[...]
    cs = (c * _sigmoid(c)).astype(BF16)
    o_ref[0] = _dot(cs, w_ref[0].astype(BF16)) + b_ref[0]


def _modulation(c, w_ada, b_ada):
    L, D, N = w_ada.shape
    B = c.shape[0]
    tn = 512
    return pl.pallas_call(
        _mod_kernel,
        grid=(L, N // tn),
        in_specs=[
            pl.BlockSpec((B, D), lambda l, j: (0, 0)),
            pl.BlockSpec((1, D, tn), lambda l, j: (l, 0, j)),
            pl.BlockSpec((1, 1, tn), lambda l, j: (l, 0, j)),
        ],
        out_specs=pl.BlockSpec((1, B, tn), lambda l, j: (l, 0, j)),
        out_shape=jax.ShapeDtypeStruct((L, B, N), F32),
        compiler_params=_cparams(2),
        name="adaln_mod",
    )(c, w_ada, b_ada.reshape(L, 1, N))


_SEG = {}
_off = 0
for _name, _w in (("qm", 512), ("km", 512), ("vm", 512), ("zm", 512), ("qn", 512),
                  ("kc", 128), ("vc", 128), ("ks", 128), ("vs", 128), ("kw", 128),
                  ("vw", 128), ("zn", 512), ("gl", 128)):
    _SEG[_name] = (_off, _w)
    _off += _w
D_IN_PACKED = _off


def _pack_w_in(w_in):
    offs = np.cumsum([0, 512, 512, 512, 512, 512, 128, 128, 128, 128, 128, 128, 24, 512])
    parts = [w_in[:, offs[i]:offs[i + 1]] for i in range(13)]
    gl, zn = parts[11], parts[12]
    gl = jnp.pad(gl, ((0, 0), (0, LANES - gl.shape[1])))
    return jnp.concatenate(parts[:11] + [zn, gl], axis=1).astype(BF16)


def _inproj_kernel(x_ref, mod_ref, g_ref, w_ref,
                   qm_ref, km_ref, vm_ref, kmean_ref, gm_ref,
                   qn_ref, ks_ref, vs_ref, kw_ref, vw_ref,
                   kc_ref, vc_ref, gate_ref, gn_ref, *, seq_len):
    tm, D = x_ref.shape
    x = x_ref[...]
    mod = mod_ref[0]
    shift, scale = mod[:, :D], mod[:, D:2 * D]
    y = x * lax.rsqrt(jnp.mean(x * x, axis=-1, keepdims=True) + RMS_EPS) * g_ref[...]
    hb = (y * (1.0 + scale) + shift).astype(BF16)

    def seg(name):
        o, w = _SEG[name]
        return _dot(hb, w_ref[:, o:o + w])

    lane = lax.broadcasted_iota(jnp.int32, (tm, LANES), 1)
    row = lax.broadcasted_iota(jnp.int32, (tm, LANES), 0)
    t = (pl.program_id(0) % (seq_len // tm)) * tm + row
    t_hi = (t // POS_RADIX).astype(F32)
    t_lo = (t % POS_RADIX).astype(F32)

    def pos_lanes(p0, v0, v1, v2, v3):
        z = jnp.zeros((tm, LANES), F32)
        return jnp.where(lane == p0, v0, jnp.where(lane == p0 + 1, v1,
                         jnp.where(lane == p0 + 2, v2, jnp.where(lane == p0 + 3, v3, z))))

    one = jnp.ones((tm, LANES), F32)
    q_pos_m = pos_lanes(MOBA_POS0, POS_RADIX * one, one, -POS_RADIX * t_hi, -t_lo)
    k_pos_m = pos_lanes(MOBA_POS0, t_hi, t_lo, one, one)
    q_pos_n = pos_lanes(NSA_POS0, POS_RADIX * one, one, -POS_RADIX * t_hi, -t_lo)
    k_pos_n = pos_lanes(NSA_POS0, t_hi, t_lo, one, one)
    k_ext_m = k_pos_m + jnp.where(lane - MOBA_SEL0 == t // MOBA_BLOCK, 1.0, 0.0)
    k_ext_s = k_pos_n + jnp.where(lane - NSA_SEL0 == t // SLC_BLOCK, 1.0, 0.0)
    v_ext = jnp.where(lane == HEAD_DIM, 1.0, 0.0)
    low = lane < HEAD_DIM

    def head(u, h, ext):
        p = u[:, (h // 2) * LANES:(h // 2 + 1) * LANES]
        if h % 2:
            p = pltpu.roll(p, HEAD_DIM, axis=1)
        return jnp.where(low, p, ext).astype(BF16)

    u = seg("qm") * (HEAD_DIM ** -0.5)
    for h in range(MOBA_HEADS):
        qm_ref[0, h] = head(u, h, _slope(h, MOBA_HEADS) * q_pos_m)
    u = seg("km")
    for h in range(MOBA_HEADS):
        km_ref[0, h] = head(u, h, k_ext_m)
    for r in range(tm // MOBA_BLOCK):
        kmean_ref[0, r] = jnp.mean(u[r * MOBA_BLOCK:(r + 1) * MOBA_BLOCK], axis=0, keepdims=True)
    u = seg("vm")
    for h in range(MOBA_HEADS):
        vm_ref[0, h] = head(u, h, v_ext)
    u = seg("zm")
    gm_ref[...] = (u * _sigmoid(u)).astype(BF16)
    u = seg("qn") * (HEAD_DIM ** -0.5)
    for h in range(NSA_HEADS):
        qn_ref[0, h] = head(u, h, _slope(h, NSA_HEADS) * q_pos_n)
    for name, ref, ext in (("ks", ks_ref, k_ext_s), ("vs", vs_ref, v_ext),
                           ("kw", kw_ref, k_pos_n), ("vw", vw_ref, v_ext)):
        u = seg(name)
        for g in range(NSA_KV_GROUPS):
            ref[0, g] = head(u, g, ext)
    kc_ref[...] = seg("kc")
    vc_ref[...] = seg("vc")
    u = seg("zn")
    gn_ref[...] = (u * _sigmoid(u)).astype(BF16)
    sg = _sigmoid(seg("gl"))
    for g in range(NSA_KV_GROUPS):
        gate_ref[0, g] = sg if g == 0 else pltpu.roll(sg, LANES - g * 3 * NSA_HPG, axis=1)


def _inproj(x2, mod, g_pre, w_packed, B, S):
    M, D = x2.shape
    tm = ROW_TILE
    nt = S // tm
    grid = (M // tm,)
    bs = lambda i: i // nt
    si = lambda i: i % nt
    head_spec = lambda nh: pl.BlockSpec((1, nh, tm, LANES), lambda i: (bs(i), 0, si(i), 0))
    row_spec = lambda w: pl.BlockSpec((tm, w), lambda i: (i, 0))
    sds = jax.ShapeDtypeStruct
    out_shape = (
        sds((B, MOBA_HEADS, S, LANES), BF16),
        sds((B, MOBA_HEADS, S, LANES), BF16),
        sds((B, MOBA_HEADS, S, LANES), BF16),
        sds((B, S // MOBA_BLOCK, 1, 512), F32),
        sds((M, 512), BF16),
        sds((B, NSA_HEADS, S, LANES), BF16),
        sds((B, NSA_KV_GROUPS, S, LANES), BF16),
        sds((B, NSA_KV_GROUPS, S, LANES), BF16),
        sds((B, NSA_KV_GROUPS, S, LANES), BF16),
        sds((B, NSA_KV_GROUPS, S, LANES), BF16),
        sds((M, LANES), F32),
        sds((M, LANES), F32),
        sds((B, NSA_KV_GROUPS, S, LANES), F32),
        sds((M, 512), BF16),
    )
    out_specs = (
        head_spec(MOBA_HEADS), head_spec(MOBA_HEADS), head_spec(MOBA_HEADS),
        pl.BlockSpec((1, tm // MOBA_BLOCK, 1, 512), lambda i: (bs(i), si(i), 0, 0)),
        row_spec(512),
        head_spec(NSA_HEADS),
        head_spec(NSA_KV_GROUPS), head_spec(NSA_KV_GROUPS),
        head_spec(NSA_KV_GROUPS), head_spec(NSA_KV_GROUPS),
        row_spec(LANES), row_spec(LANES),
        head_spec(NSA_KV_GROUPS),
        row_spec(512),
    )
    return pl.pallas_call(
        functools.partial(_inproj_kernel, seq_len=S),
        grid=grid,
        in_specs=[
            row_spec(D),
            pl.BlockSpec((1, 1, 3 * D), lambda i: (bs(i), 0, 0)),
            pl.BlockSpec((1, D), lambda i: (0, 0)),
            pl.BlockSpec((D, D_IN_PACKED), lambda i: (0, 0)),
        ],
        out_specs=out_specs,
        out_shape=out_shape,
        compiler_params=_cparams(1),
        name="in_proj",
    )(x2, mod, g_pre, w_packed)


def _gelu_tanh(x):
    return 0.5 * x * (1.0 + jnp.tanh(np.sqrt(2.0 / np.pi) * (x + 0.044715 * (x * x * x))))


def _compress_kernel(kc_ref, vc_ref, pek_ref, pev_ref, wk1_ref, wk2_ref, wv1_ref, wv2_ref,
                     m_ref, kout_ref, vout_ref, win_ref):
    S = kc_ref.shape[0]
    nrow = S // CMP_STRIDE
    lane = lax.broadcasted_iota(jnp.int32, (nrow, LANES), 1)
    row = lax.broadcasted_iota(jnp.int32, (nrow, LANES), 0)
    low = lane < HEAD_DIM
    valid = row < nrow - 1
    k_ext = jnp.where(lane == NSA_POS0, (row + 1).astype(F32),
                      jnp.where(lane == NSA_POS0 + 1, float(POS_RADIX - 1),
                                jnp.where((lane == NSA_POS0 + 2) | (lane == NSA_POS0 + 3), 1.0, 0.0)))

    def run(src_ref, pe_ref, w1_ref, w2_ref, out_ref, ext, transposed):
        for l in range(CMP_STRIDE):
            win_ref[:, l * LANES:(l + 1) * LANES] = src_ref[pl.ds(l, nrow, stride=CMP_STRIDE), :]
        xw = win_ref[...]
        lo = _dot((xw + pe_ref[0:1, :]).astype(BF16), w1_ref[0])
        hi = _dot((xw + pe_ref[1:2, :]).astype(BF16), w1_ref[1])
        pre = lo + pltpu.roll(hi, nrow - 1, axis=0)
        out = _dot(_gelu_tanh(pre).astype(BF16), w2_ref[...])
        out = jnp.where(valid, out, 0.0)
        for g in range(NSA_KV_GROUPS):
            p = out if g == 0 else pltpu.roll(out, HEAD_DIM, axis=1)
            p = jnp.where(low, p, ext)
            out_ref[0, g] = (p.T if transposed else p).astype(BF16)

    run(kc_ref, pek_ref, wk1_ref, wk2_ref, kout_ref, k_ext, False)
    run(vc_ref, pev_ref, wv1_ref, wv2_ref, vout_ref, jnp.where(valid, m_ref[...], 0.0), True)


def _pack_compress_weights(pe, w1, w2):
    half = CMP_STRIDE
    eye = jnp.eye(NSA_KV_GROUPS, dtype=F32)
    w1r = w1.reshape(2, half, HEAD_DIM, HEAD_DIM)
    w1bd = jnp.einsum("hlde,gk->hlgdke", w1r, eye).reshape(2, half * LANES, LANES).astype(BF16)
    w2bd = jnp.einsum("de,gk->gdke", w2, eye).reshape(LANES, LANES).astype(BF16)
    pet = jnp.tile(pe.reshape(2, half, 1, HEAD_DIM), (1, 1, NSA_KV_GROUPS, 1)).reshape(2, half * LANES)
    return pet, w1bd, w2bd


def _cmp_to_slc_lanes(n_rows, n_slc):
    i = np.arange(n_rows)[:, None]
    j = np.arange(n_slc)[None, :]
    start = i * CMP_STRIDE
    ov = (start < (j + 1) * SLC_BLOCK) & (start + CMP_LEN > j * SLC_BLOCK)
    m = np.zeros((n_rows, LANES), np.float32)
    m[:, NSA_SEL0:NSA_SEL0 + n_slc] = ov
    return jnp.asarray(m)


def _compress(kc, vc, pk, pv, B, S):
    nrow = S // CMP_STRIDE
    pek, wk1, wk2 = pk
    pev, wv1, wv2 = pv
    m = _cmp_to_slc_lanes(nrow, S // SLC_BLOCK)
    full = lambda a: pl.BlockSpec(a.shape, lambda b: (0,) * a.ndim)
    out_spec = pl.BlockSpec((1, NSA_KV_GROUPS, nrow, LANES), lambda b: (b, 0, 0, 0))
    out_sds = jax.ShapeDtypeStruct((B, NSA_KV_GROUPS, nrow, LANES), BF16)
    return pl.pallas_call(
        _compress_kernel,
        grid=(B,),
        in_specs=[pl.BlockSpec((S, LANES), lambda b: (b, 0)), pl.BlockSpec((S, LANES), lambda b: (b, 0)),
                  full(pek), full(pev), full(wk1), full(wk2), full(wv1), full(wv2), full(m)],
        out_specs=(out_spec, out_spec),
        out_shape=(out_sds, out_sds),
        scratch_shapes=[pltpu.VMEM((nrow, CMP_STRIDE * LANES), F32)],
        compiler_params=_cparams(1),
        name="nsa_compress",
    )(kc, vc, pek, pev, wk1, wk2, wv1, wv2, m)


def _rank_rows(vals, count):
    idx = lax.broadcasted_iota(jnp.int32, vals.shape, 0)
    rank = jnp.zeros(vals.shape, jnp.int32)
    for j in range(count):
        rowj = vals[j:j + 1, :]
        beats = (rowj > vals) | ((rowj == vals) & (j < idx))
        rank = rank + beats.astype(jnp.int32)
    return rank


def _tile_masks():
    kpos = lax.broadcasted_iota(jnp.int32, (TILE, TILE), 0)
    qpos = lax.broadcasted_iota(jnp.int32, (TILE, TILE), 1)
    return kpos <= qpos, kpos > qpos


def _tile_slice(j):
    return pl.ds(pl.multiple_of(j * TILE, TILE), TILE)


def _scores(k_of, qts, j, mask=None):
    out = []
    for h, qt in enumerate(qts):
        s = _dot(k_of(h, j), qt)
        out.append(s if mask is None else jnp.where(mask, s, NEG))
    return tuple(out)


def _absorb(state, s, vt_of, j):
    new = []
    for h, (m, acc) in enumerate(state):
        m_new = jnp.maximum(m, jnp.max(s[h], axis=0, keepdims=True))
        p = jnp.exp(s[h] - m_new).astype(BF16)
        new.append((m_new, jnp.exp(m - m_new) * acc + _dot(vt_of(h, j), p)))
    return tuple(new)


def _init_state(n_heads):
    return tuple((jnp.full((1, TILE), jnp.finfo(F32).min, F32), jnp.zeros((LANES, TILE), F32))
                 for _ in range(n_heads))


def _causal_sweep(qts, k_of, vt_of, qi, causal):
    def body(j, carry):
        s, state = carry
        s_next = _scores(k_of, qts, j)
        return s_next, _absorb(state, s, vt_of, jnp.where(j == 0, qi, j - 1))

    s, state = lax.fori_loop(0, qi, body, (_scores(k_of, qts, qi, causal), _init_state(len(qts))))
    return _absorb(state, s, vt_of, jnp.maximum(qi - 1, 0))


def _normalize(carry):
    acc = carry[1]
    return acc[:HEAD_DIM] / acc[HEAD_DIM:HEAD_DIM + 1]


def _fill_transposed(vt_ref, v_ref, n_tiles):
    for c in range(n_tiles):
        vt_ref[c] = v_ref[c * TILE:(c + 1) * TILE, :].T


def _moba_kernel(q_ref, k_ref, v_ref, kmean_ref, gate_ref, o_ref, vt_ref):
    qi = pl.program_id(2)
    n_tiles = k_ref.shape[2] // TILE

    @pl.when(qi == 0)
    def _():
        for h in range(2):
            _fill_transposed(vt_ref.at[h], v_ref.at[0, h], n_tiles)

    blk = lax.broadcasted_iota(jnp.int32, (16, TILE), 0)
    causal, _ = _tile_masks()
    qts = []
    for h in range(2):
        qt = q_ref[0, h].T
        gsc = jnp.where(blk < qi, _dot(kmean_ref[0, h], qt), NEG)
        rank = _rank_rows(gsc, 8)
        keep = ((blk < qi) & (rank < jnp.minimum(MOBA_TOPK, qi))) | (blk == qi)
        ext = qt[MOBA_SEL0:MOBA_SEL0 + 16].astype(F32)
        ext = jnp.where((blk < 8) & jnp.logical_not(keep), NEG, ext).astype(BF16)
        qts.append(jnp.concatenate([qt[:MOBA_SEL0], ext, qt[MOBA_SEL0 + 16:]], axis=0))
    carry = _causal_sweep(qts, lambda h, j: k_ref[0, h, _tile_slice(j), :],
                          lambda h, j: vt_ref[h, j], qi, causal)
    o = jnp.concatenate([_normalize(c) for c in carry], axis=0)
    o_ref[...] = (o.T * gate_ref[...].astype(F32)).astype(BF16)


def _moba(qm, km, vm, kmean, gm, B, S):
    nq = S // TILE
    return pl.pallas_call(
        _moba_kernel,
        grid=(B, MOBA_HEADS // 2, nq),
        in_specs=[
            pl.BlockSpec((1, 2, TILE, LANES), lambda b, p, i: (b, p, i, 0)),
            pl.BlockSpec((1, 2, S, LANES), lambda b, p, i: (b, p, 0, 0)),
            pl.BlockSpec((1, 2, S, LANES), lambda b, p, i: (b, p, 0, 0)),
            pl.BlockSpec((1, 2, 16, LANES), lambda b, p, i: (b, p, 0, 0)),
            pl.BlockSpec((TILE, LANES), lambda b, p, i: (b * nq + i, p)),
        ],
        out_specs=pl.BlockSpec((TILE, LANES), lambda b, p, i: (b * nq + i, p)),
        out_shape=jax.ShapeDtypeStruct((B * S, MOBA_HEADS * HEAD_DIM), BF16),
        scratch_shapes=[pltpu.VMEM((2, nq, LANES, TILE), BF16)],
        compiler_params=_cparams(3),
        name="moba_attn",
    )(qm, km, vm, kmean, gm)


def _nsa_kernel(q_ref, ks_ref, vs_ref, kw_ref, vw_ref, kc_ref, vct_ref, bg_ref, gate_ref, o_ref,
                vst_ref, vwt_ref):
    qi = pl.program_id(2)
    n_tiles = ks_ref.shape[2] // TILE
    heads = range(NSA_HPG)

    @pl.when(qi == 0)
    def _():
        _fill_transposed(vst_ref, vs_ref.at[0, 0], n_tiles)
        _fill_transposed(vwt_ref, vw_ref.at[0, 0], n_tiles)

    causal, anti = _tile_masks()
    qts = [q_ref[0, j].T for j in heads]

    n_cmp = kc_ref.shape[2]
    t_c = qi * TILE + lax.broadcasted_iota(jnp.int32, (n_cmp, TILE), 1)
    ok_c = t_c >= lax.broadcasted_iota(jnp.int32, (n_cmp, TILE), 0) * CMP_STRIDE + (CMP_LEN - 1)
    kc, vct = kc_ref[0, 0], vct_ref[0, 0]
    r_cmp = []
    for j in heads:
        s = jnp.where(ok_c, _dot(kc, qts[j]), NEG)
        e = jnp.exp(s - jnp.max(s, axis=0, keepdims=True))
        p = jnp.where(ok_c, e / jnp.sum(e, axis=0, keepdims=True), 0.0)
        r_cmp.append(_dot(vct, p.astype(BF16)))
    n_sel = NSA_POS0 - NSA_SEL0
    imp = ((r_cmp[0][NSA_SEL0:NSA_POS0] + r_cmp[1][NSA_SEL0:NSA_POS0])
           + (r_cmp[2][NSA_SEL0:NSA_POS0] + r_cmp[3][NSA_SEL0:NSA_POS0]))

    n = lax.broadcasted_iota(jnp.int32, (n_sel, TILE), 0)
    blk = (qi * TILE + lax.broadcasted_iota(jnp.int32, (n_sel, TILE), 1)) // SLC_BLOCK
    forced = (n == 0) | (n == blk) | (n == blk - 1)
    imp = jnp.where(forced, FORCE_BONUS, imp)
    imp = jnp.where(n <= blk, imp, NEG)
    keep = (_rank_rows(imp, n_sel) < SLC_TOPN) & (n <= blk)
    bias = jnp.where(keep, 0.0, NEG).astype(BF16)
    qsel = [jnp.concatenate([qts[j][:NSA_SEL0], bias, qts[j][NSA_POS0:]], axis=0) for j in heads]

    sel = _causal_sweep(qsel, lambda h, j: ks_ref[0, 0, _tile_slice(j), :],
                        lambda h, j: vst_ref[j], qi, causal)

    wk = lambda h, j: kw_ref[0, 0, _tile_slice(jnp.maximum(j, 0)), :]
    wv = lambda h, j: vwt_ref[jnp.maximum(j, 0)]
    s0 = _scores(wk, qts, qi, causal)
    s1 = _scores(wk, qts, qi - 1, qi >= 1)
    win = _absorb(_init_state(NSA_HPG), s0, wv, qi)
    s2 = _scores(wk, qts, qi - 2, jnp.logical_and(anti, qi >= 2))
    win = _absorb(win, s1, wv, qi - 1)
    win = _absorb(win, s2, wv, qi - 2)

    bg = bg_ref[0, 0].T
    outs = [bg[3 * j:3 * j + 1] * r_cmp[j][:HEAD_DIM] + bg[3 * j + 1:3 * j + 2] * _normalize(sel[j])
            + bg[3 * j + 2:3 * j + 3] * _normalize(win[j]) for j in heads]
    o = jnp.concatenate(outs, axis=0)
    o_ref[...] = (o.T * gate_ref[...].astype(F32)).astype(BF16)


def _nsa(qn, ks, vs, kw, vw, kcmp, vcmp_t, gates, gn, B, S):
    nq = S // TILE
    nrow = S // CMP_STRIDE
    kv_spec = pl.BlockSpec((1, 1, S, LANES), lambda b, g, i: (b, g, 0, 0))
    cmp_spec = pl.BlockSpec((1, 1, nrow, LANES), lambda b, g, i: (b, g, 0, 0))
    wide = NSA_HPG * HEAD_DIM
    return pl.pallas_call(
        _nsa_kernel,
        grid=(B, NSA_KV_GROUPS, nq),
        in_specs=[
            pl.BlockSpec((1, NSA_HPG, TILE, LANES), lambda b, g, i: (b, g, i, 0)),
            kv_spec, kv_spec, kv_spec, kv_spec, cmp_spec, cmp_spec,
            pl.BlockSpec((1, 1, TILE, LANES), lambda b, g, i: (b, g, i, 0)),
            pl.BlockSpec((TILE, wide), lambda b, g, i: (b * nq + i, g)),
        ],
        out_specs=pl.BlockSpec((TILE, wide), lambda b, g, i: (b * nq + i, g)),
        out_shape=jax.ShapeDtypeStruct((B * S, NSA_HEADS * HEAD_DIM), BF16),
        scratch_shapes=[pltpu.VMEM((nq, LANES, TILE), BF16), pltpu.VMEM((nq, LANES, TILE), BF16)],
        compiler_params=_cparams(3),
        name="nsa_attn",
    )(qn, ks, vs, kw, vw, kcmp, vcmp_t, gates, gn)


def _outproj_kernel(ym_ref, yn_ref, w_ref, x_ref, mod_ref, g_ref, o_ref):
    D = x_ref.shape[1]
    half = ym_ref.shape[1]
    z = _dot(ym_ref[...], w_ref[:half, :]) + _dot(yn_ref[...], w_ref[half:, :])
    r = z * lax.rsqrt(jnp.mean(z * z, axis=-1, keepdims=True) + RMS_EPS) * g_ref[...]
    o_ref[...] = x_ref[...] + mod_ref[0][:, 2 * D:] * r


def _outproj(ym, yn, w_out, x2, mod, g_post, S):
    M, D = x2.shape
    tm = ROW_TILE
    nt = S // tm
    row_spec = lambda w: pl.BlockSpec((tm, w), lambda i: (i, 0))
    return pl.pallas_call(
        _outproj_kernel,
        grid=(M // tm,),
        in_specs=[
            row_spec(ym.shape[1]), row_spec(yn.shape[1]),
            pl.BlockSpec(w_out.shape, lambda i: (0, 0)),
            row_spec(D),
            pl.BlockSpec((1, 1, 3 * D), lambda i: (i // nt, 0, 0)),
            pl.BlockSpec((1, D), lambda i: (0, 0)),
        ],
        out_specs=row_spec(D),
        out_shape=jax.ShapeDtypeStruct((M, D), F32),
        compiler_params=_cparams(1),
        name="out_proj",
    )(ym, yn, w_out, x2, mod, g_post)


def kernel(x, c, w_ada, b_ada, g_pre, g_post, w_in, w_out, pe_k, pe_v, w_ck1, w_ck2, w_cv1, w_cv2):
    B, S, D = x.shape
    L = w_ada.shape[0]
    assert D == (MOBA_HEADS + NSA_HEADS) * HEAD_DIM and S % ROW_TILE == 0
    assert S // MOBA_BLOCK <= 8 and S // SLC_BLOCK <= 32 and S // CMP_STRIDE <= LANES
    nb = S // MOBA_BLOCK

    mod_all = _modulation(c, w_ada, b_ada)
    x2 = x.reshape(B * S, D)
    for l in range(L):
        mod = mod_all[l].reshape(B, 1, 3 * D)
        (qm, km, vm, kmean, gm, qn, ks, vs, kw, vw, kc, vc, gates, gn) = _inproj(
            x2, mod, g_pre[l].reshape(1, D), _pack_w_in(w_in[l]), B, S)
        kmean = kmean.reshape(B, nb, MOBA_HEADS, HEAD_DIM).transpose(0, 2, 1, 3)
        kmean = jnp.pad(kmean, ((0, 0), (0, 0), (0, 16 - nb), (0, LANES - HEAD_DIM))).astype(BF16)
        kcmp, vcmp = _compress(kc, vc, _pack_compress_weights(pe_k[l], w_ck1[l], w_ck2[l]),
                               _pack_compress_weights(pe_v[l], w_cv1[l], w_cv2[l]), B, S)
        ym = _moba(qm, km, vm, kmean, gm, B, S)
        yn = _nsa(qn, ks, vs, kw, vw, kcmp, vcmp, gates, gn, B, S)
        x2 = _outproj(ym, yn, w_out[l].astype(BF16), x2, mod, g_post[l].reshape(1, D), S)
    return x2.reshape(B, S, D)
```

```python
import functools

import numpy as np
import jax
import jax.numpy as jnp
from jax import lax
from jax.experimental import pallas as pl
from jax.experimental.pallas import tpu as pltpu

HEAD_DIM = 64
MOBA_HEADS = 8
NSA_HEADS = 8
NSA_KV_GROUPS = 2
NSA_HPG = NSA_HEADS // NSA_KV_GROUPS
MOBA_BLOCK = 256
MOBA_TOPK = 3
CMP_LEN = 32
CMP_STRIDE = 16
SLC_BLOCK = 64
SLC_TOPN = 16
WINDOW = 512
RMS_EPS = 1e-6
NEG = -1e9
FORCE_BONUS = 1e4

LANES = 128
TILE = 256
POS_RADIX = 16
ROW_TILE = 512
VMEM_LIMIT = 56 * 1024 * 1024

MOBA_SEL0 = HEAD_DIM
MOBA_POS0 = MOBA_SEL0 + 8
NSA_SEL0 = HEAD_DIM
NSA_POS0 = NSA_SEL0 + 32

BF16 = jnp.bfloat16
F32 = jnp.float32


def _slope(i, n):
    return 2.0 ** (-8.0 * (i + 1) / n)


def _dot(a, b):
    return jnp.dot(a, b, preferred_element_type=F32)


def _sigmoid(x):
    return 1.0 / (1.0 + jnp.exp(-x))


def _cparams(n_axes):
    return pltpu.CompilerParams(
        dimension_semantics=("arbitrary",) * n_axes, vmem_limit_bytes=VMEM_LIMIT)


def _mod_kernel(c_ref, w_ref, b_ref, o_ref):
    c = c_ref[...]
    cs = (c * _sigmoid(c)).astype(BF16)
    o_ref[0] = _dot(cs, w_ref[0].astype(BF16)) + b_ref[0]


def _modulation(c, w_ada, b_ada):
    L, D, N = w_ada.shape
    B = c.shape[0]
    tn = 512
    return pl.pallas_call(
        _mod_kernel,
        grid=(L, N // tn),
        in_specs=[
            pl.BlockSpec((B, D), lambda l, j: (0, 0)),
            pl.BlockSpec((1, D, tn), lambda l, j: (l, 0, j)),
            pl.BlockSpec((1, 1, tn), lambda l, j: (l, 0, j)),
        ],
        out_specs=pl.BlockSpec((1, B, tn), lambda l, j: (l, 0, j)),
        out_shape=jax.ShapeDtypeStruct((L, B, N), F32),
        compiler_params=_cparams(2),
        name="adaln_mod",
    )(c, w_ada, b_ada.reshape(L, 1, N))


_SEG = {}
_off = 0
for _name, _w in (("qm", 512), ("km", 512), ("vm", 512), ("zm", 512), ("qn", 512),
                  ("kc", 128), ("vc", 128), ("ks", 128), ("vs", 128), ("kw", 128),
                  ("vw", 128), ("zn", 512), ("gl", 128)):
    _SEG[_name] = (_off, _w)
    _off += _w
D_IN_PACKED = _off


def _pack_w_in(w_in):
    offs = np.cumsum([0, 512, 512, 512, 512, 512, 128, 128, 128, 128, 128, 128, 24, 512])
    parts = [w_in[:, offs[i]:offs[i + 1]] for i in range(13)]
    gl, zn = parts[11], parts[12]
    gl = jnp.pad(gl, ((0, 0), (0, LANES - gl.shape[1])))
    return jnp.concatenate(parts[:11] + [zn, gl], axis=1).astype(BF16)


def _inproj_kernel(x_ref, mod_ref, g_ref, w_ref,
                   qm_ref, km_ref, vm_ref, kmean_ref, gm_ref,
                   qn_ref, ks_ref, vs_ref, kw_ref, vw_ref,
                   kc_ref, vc_ref, gate_ref, gn_ref, *, seq_len):
    tm, D = x_ref.shape
    x = x_ref[...]
    mod = mod_ref[0]
    shift, scale = mod[:, :D], mod[:, D:2 * D]
    y = x * lax.rsqrt(jnp.mean(x * x, axis=-1, keepdims=True) + RMS_EPS) * g_ref[...]
    hb = (y * (1.0 + scale) + shift).astype(BF16)

    def seg(name):
        o, w = _SEG[name]
        return _dot(hb, w_ref[:, o:o + w])

    lane = lax.broadcasted_iota(jnp.int32, (tm, LANES), 1)
    row = lax.broadcasted_iota(jnp.int32, (tm, LANES), 0)
    t = (pl.program_id(0) % (seq_len // tm)) * tm + row
    t_hi = (t // POS_RADIX).astype(F32)
    t_lo = (t % POS_RADIX).astype(F32)

    def pos_lanes(p0, v0, v1, v2, v3):
        z = jnp.zeros((tm, LANES), F32)
        return jnp.where(lane == p0, v0, jnp.where(lane == p0 + 1, v1,
                         jnp.where(lane == p0 + 2, v2, jnp.where(lane == p0 + 3, v3, z))))

    one = jnp.ones((tm, LANES), F32)
    q_pos_m = pos_lanes(MOBA_POS0, POS_RADIX * one, one, -POS_RADIX * t_hi, -t_lo)
    k_pos_m = pos_lanes(MOBA_POS0, t_hi, t_lo, one, one)
    q_pos_n = pos_lanes(NSA_POS0, POS_RADIX * one, one, -POS_RADIX * t_hi, -t_lo)
    k_pos_n = pos_lanes(NSA_POS0, t_hi, t_lo, one, one)
    k_ext_m = k_pos_m + jnp.where(lane - MOBA_SEL0 == t // MOBA_BLOCK, 1.0, 0.0)
    k_ext_s = k_pos_n + jnp.where(lane - NSA_SEL0 == t // SLC_BLOCK, 1.0, 0.0)
    v_ext = jnp.where(lane == HEAD_DIM, 1.0, 0.0)
    low = lane < HEAD_DIM

    def head(u, h, ext):
        p = u[:, (h // 2) * LANES:(h // 2 + 1) * LANES]
        if h % 2:
            p = pltpu.roll(p, HEAD_DIM, axis=1)
        return jnp.where(low, p, ext).astype(BF16)

    u = seg("qm") * (HEAD_DIM ** -0.5)
    for h in range(MOBA_HEADS):
        qm_ref[0, h] = head(u, h, _slope(h, MOBA_HEADS) * q_pos_m)
    u = seg("km")
    for h in range(MOBA_HEADS):
        km_ref[0, h] = head(u, h, k_ext_m)
    for r in range(tm // MOBA_BLOCK):
        kmean_ref[0, r] = jnp.mean(u[r * MOBA_BLOCK:(r + 1) * MOBA_BLOCK], axis=0, keepdims=True)
    u = seg("vm")
    for h in range(MOBA_HEADS):
        vm_ref[0, h] = head(u, h, v_ext)
    u = seg("zm")
    gm_ref[...] = (u * _sigmoid(u)).astype(BF16)
    u = seg("qn") * (HEAD_DIM ** -0.5)
    for h in range(NSA_HEADS):
        qn_ref[0, h] = head(u, h, _slope(h, NSA_HEADS) * q_pos_n)
    for name, ref, ext in (("ks", ks_ref, k_ext_s), ("vs", vs_ref, v_ext),
                           ("kw", kw_ref, k_pos_n), ("vw", vw_ref, v_ext)):
        u = seg(name)
        for g in range(NSA_KV_GROUPS):
            ref[0, g] = head(u, g, ext)
    kc_ref[...] = seg("kc")
    vc_ref[...] = seg("vc")
    u = seg("zn")
    gn_ref[...] = (u * _sigmoid(u)).astype(BF16)
    sg = _sigmoid(seg("gl"))
    for g in range(NSA_KV_GROUPS):
        gate_ref[0, g] = sg if g == 0 else pltpu.roll(sg, LANES - g * 3 * NSA_HPG, axis=1)


def _inproj(x2, mod, g_pre, w_packed, B, S):
    M, D = x2.shape
    tm = ROW_TILE
    nt = S // tm
    grid = (M // tm,)
    bs = lambda i: i // nt
    si = lambda i: i % nt
    head_spec = lambda nh: pl.BlockSpec((1, nh, tm, LANES), lambda i: (bs(i), 0, si(i), 0))
    row_spec = lambda w: pl.BlockSpec((tm, w), lambda i: (i, 0))
    sds = jax.ShapeDtypeStruct
    out_shape = (
        sds((B, MOBA_HEADS, S, LANES), BF16),
        sds((B, MOBA_HEADS, S, LANES), BF16),
        sds((B, MOBA_HEADS, S, LANES), BF16),
        sds((B, S // MOBA_BLOCK, 1, 512), F32),
        sds((M, 512), BF16),
        sds((B, NSA_HEADS, S, LANES), BF16),
        sds((B, NSA_KV_GROUPS, S, LANES), BF16),
        sds((B, NSA_KV_GROUPS, S, LANES), BF16),
        sds((B, NSA_KV_GROUPS, S, LANES), BF16),
        sds((B, NSA_KV_GROUPS, S, LANES), BF16),
        sds((M, LANES), F32),
        sds((M, LANES), F32),
        sds((B, NSA_KV_GROUPS, S, LANES), F32),
        sds((M, 512), BF16),
    )
    out_specs = (
        head_spec(MOBA_HEADS), head_spec(MOBA_HEADS), head_spec(MOBA_HEADS),
        pl.BlockSpec((1, tm // MOBA_BLOCK, 1, 512), lambda i: (bs(i), si(i), 0, 0)),
        row_spec(512),
        head_spec(NSA_HEADS),
        head_spec(NSA_KV_GROUPS), head_spec(NSA_KV_GROUPS),
        head_spec(NSA_KV_GROUPS), head_spec(NSA_KV_GROUPS),
        row_spec(LANES), row_spec(LANES),
        head_spec(NSA_KV_GROUPS),
        row_spec(512),
    )
    return pl.pallas_call(
        functools.partial(_inproj_kernel, seq_len=S),
        grid=grid,
        in_specs=[
            row_spec(D),
            pl.BlockSpec((1, 1, 3 * D), lambda i: (bs(i), 0, 0)),
            pl.BlockSpec((1, D), lambda i: (0, 0)),
            pl.BlockSpec((D, D_IN_PACKED), lambda i: (0, 0)),
        ],
        out_specs=out_specs,
        out_shape=out_shape,
        compiler_params=_cparams(1),
        name="in_proj",
    )(x2, mod, g_pre, w_packed)


def _gelu_tanh(x):
    return 0.5 * x * (1.0 + jnp.tanh(np.sqrt(2.0 / np.pi) * (x + 0.044715 * (x * x * x))))


def _compress_kernel(kc_ref, vc_ref, pek_ref, pev_ref, wk1_ref, wk2_ref, wv1_ref, wv2_ref,
                     m_ref, kout_ref, vout_ref, win_ref):
    S = kc_ref.shape[0]
    nrow = S // CMP_STRIDE
    lane = lax.broadcasted_iota(jnp.int32, (nrow, LANES), 1)
    row = lax.broadcasted_iota(jnp.int32, (nrow, LANES), 0)
    low = lane < HEAD_DIM
    valid = row < nrow - 1
    k_ext = jnp.where(lane == NSA_POS0, (row + 1).astype(F32),
                      jnp.where(lane == NSA_POS0 + 1, float(POS_RADIX - 1),
                                jnp.where((lane == NSA_POS0 + 2) | (lane == NSA_POS0 + 3), 1.0, 0.0)))

    def run(src_ref, pe_ref, w1_ref, w2_ref, out_ref, ext, transposed):
        for l in range(CMP_STRIDE):
            win_ref[:, l * LANES:(l + 1) * LANES] = src_ref[pl.ds(l, nrow, stride=CMP_STRIDE), :]
        xw = win_ref[...]
        lo = _dot((xw + pe_ref[0:1, :]).astype(BF16), w1_ref[0])
        hi = _dot((xw + pe_ref[1:2, :]).astype(BF16), w1_ref[1])
        pre = lo + pltpu.roll(hi, nrow - 1, axis=0)
        out = _dot(_gelu_tanh(pre).astype(BF16), w2_ref[...])
        out = jnp.where(valid, out, 0.0)
        for g in range(NSA_KV_GROUPS):
            p = out if g == 0 else pltpu.roll(out, HEAD_DIM, axis=1)
            p = jnp.where(low, p, ext)
            out_ref[0, g] = (p.T if transposed else p).astype(BF16)

    run(kc_ref, pek_ref, wk1_ref, wk2_ref, kout_ref, k_ext, False)
    run(vc_ref, pev_ref, wv1_ref, wv2_ref, vout_ref, jnp.where(valid, m_ref[...], 0.0), True)


def _pack_compress_weights(pe, w1, w2):
    half = CMP_STRIDE
    eye = jnp.eye(NSA_KV_GROUPS, dtype=F32)
    w1r = w1.reshape(2, half, HEAD_DIM, HEAD_DIM)
    w1bd = jnp.einsum("hlde,gk->hlgdke", w1r, eye).reshape(2, half * LANES, LANES).astype(BF16)
    w2bd = jnp.einsum("de,gk->gdke", w2, eye).reshape(LANES, LANES).astype(BF16)
    pet = jnp.tile(pe.reshape(2, half, 1, HEAD_DIM), (1, 1, NSA_KV_GROUPS, 1)).reshape(2, half * LANES)
    return pet, w1bd, w2bd


def _cmp_to_slc_lanes(n_rows, n_slc):
    i = np.arange(n_rows)[:, None]
    j = np.arange(n_slc)[None, :]
    start = i * CMP_STRIDE
    ov = (start < (j + 1) * SLC_BLOCK) & (start + CMP_LEN > j * SLC_BLOCK)
    m = np.zeros((n_rows, LANES), np.float32)
    m[:, NSA_SEL0:NSA_SEL0 + n_slc] = ov
    return jnp.asarray(m)


def _compress(kc, vc, pk, pv, B, S):
    nrow = S // CMP_STRIDE
    pek, wk1, wk2 = pk
    pev, wv1, wv2 = pv
    m = _cmp_to_slc_lanes(nrow, S // SLC_BLOCK)
    full = lambda a: pl.BlockSpec(a.shape, lambda b: (0,) * a.ndim)
    out_spec = pl.BlockSpec((1, NSA_KV_GROUPS, nrow, LANES), lambda b: (b, 0, 0, 0))
    out_sds = jax.ShapeDtypeStruct((B, NSA_KV_GROUPS, nrow, LANES), BF16)
    return pl.pallas_call(
        _compress_kernel,
        grid=(B,),
        in_specs=[pl.BlockSpec((S, LANES), lambda b: (b, 0)), pl.BlockSpec((S, LANES), lambda b: (b, 0)),
                  full(pek), full(pev), full(wk1), full(wk2), full(wv1), full(wv2), full(m)],
        out_specs=(out_spec, out_spec),
        out_shape=(out_sds, out_sds),
        scratch_shapes=[pltpu.VMEM((nrow, CMP_STRIDE * LANES), F32)],
        compiler_params=_cparams(1),
        name="nsa_compress",
    )(kc, vc, pek, pev, wk1, wk2, wv1, wv2, m)


def _rank_rows(vals, count):
    idx = lax.broadcasted_iota(jnp.int32, vals.shape, 0)
    rank = jnp.zeros(vals.shape, jnp.int32)
    for j in range(count):
        rowj = vals[j:j + 1, :]
        beats = (rowj > vals) | ((rowj == vals) & (j < idx))
        rank = rank + beats.astype(jnp.int32)
    return rank


def _tile_masks():
    kpos = lax.broadcasted_iota(jnp.int32, (TILE, TILE), 0)
    qpos = lax.broadcasted_iota(jnp.int32, (TILE, TILE), 1)
    return kpos <= qpos, kpos > qpos


def _tile_slice(j):
    return pl.ds(pl.multiple_of(j * TILE, TILE), TILE)


HALF = TILE // 2
V_ROWS = 80
QK_LEAD = 5
ACC_LAG = 2


def _attend(items, state):
    n = len(items)
    state = list(state)
    scores, updates = {}, {}
    for t in range(n + QK_LEAD + ACC_LAG):
        if t < n:
            _, qt, k, _, mask = items[t]
            s = _dot(k, qt)
            scores[t] = s if mask is None else jnp.where(mask, s, NEG)
        i = t - QK_LEAD
        if 0 <= i < n:
            slot, _, _, vt, _ = items[i]
            s, m, parts = scores.pop(i), state[slot][0], []
            for half in range(TILE // HALF):
                sh = s[half * HALF:(half + 1) * HALF]
                m_new = jnp.maximum(m, jnp.max(sh, axis=0, keepdims=True))
                p = jnp.exp(sh - m_new).astype(BF16)
                parts.append((jnp.exp(m - m_new), _dot(vt[:, half * HALF:(half + 1) * HALF], p)))
                m = m_new
            state[slot] = (m, state[slot][1])
            updates[i] = parts
        i = t - QK_LEAD - ACC_LAG
        if 0 <= i < n:
            slot = items[i][0]
            acc = state[slot][1]
            for alpha, pv in updates.pop(i):
                acc = alpha * acc + pv
            state[slot] = (state[slot][0], acc)
    return tuple(state)


def _init_state(n_slots):
    return tuple((jnp.full((1, TILE), jnp.finfo(F32).min, F32), jnp.zeros((V_ROWS, TILE), F32))
                 for _ in range(n_slots))


def _tile_items(qts, k_of, vt_of, j, mask=None, slot0=0):
    return [(slot0 + h, qt, k_of(h, j), vt_of(h, j), mask) for h, qt in enumerate(qts)]


def _past_sweep(state, qts, k_of, vt_of, qi):
    tiles = lambda *js: sum((_tile_items(qts, k_of, vt_of, j) for j in js), [])
    state = lax.fori_loop(0, qi // 2, lambda i, st: _attend(tiles(2 * i, 2 * i + 1), st), state)
    return lax.cond(qi % 2 == 1, lambda st: _attend(tiles(qi - 1), st), lambda st: st, state)


def _normalize(carry):
    acc = carry[1]
    return acc[:HEAD_DIM] / acc[HEAD_DIM:HEAD_DIM + 1]


def _fill_transposed(vt_ref, v_ref, n_tiles):
    for c in range(n_tiles):
        vt_ref[c] = v_ref[c * TILE:(c + 1) * TILE, :].T


MOBA_HPS = 8


def _moba_kernel(q_ref, k_ref, v_ref, kmean_ref, gate_ref, o_ref, vt_ref):
    qi = pl.program_id(2)
    n_tiles = k_ref.shape[2] // TILE
    heads = range(q_ref.shape[1])

    @pl.when(qi == 0)
    def _():
        for h in heads:
            _fill_transposed(vt_ref.at[h], v_ref.at[0, h], n_tiles)

    blk = lax.broadcasted_iota(jnp.int32, (16, TILE), 0)
    causal, _ = _tile_masks()
    qts = [q_ref[0, h].T for h in heads]
    gscs = [jnp.where(blk < qi, _dot(kmean_ref[0, h], qts[h]), NEG) for h in heads]
    qsel = []
    for h in heads:
        keep = ((blk < qi) & (_rank_rows(gscs[h], 8) < jnp.minimum(MOBA_TOPK, qi))) | (blk == qi)
        ext = qts[h][MOBA_SEL0:MOBA_SEL0 + 16].astype(F32)
        ext = jnp.where((blk < 8) & jnp.logical_not(keep), NEG, ext).astype(BF16)
        qsel.append(jnp.concatenate([qts[h][:MOBA_SEL0], ext, qts[h][MOBA_SEL0 + 16:]], axis=0))
    k_of = lambda h, j: k_ref[0, h, _tile_slice(j), :]
    vt_of = lambda h, j: vt_ref[h, j, :V_ROWS, :]
    state = _attend(_tile_items(qsel, k_of, vt_of, qi, causal), _init_state(len(qsel)))
    state = _past_sweep(state, qsel, k_of, vt_of, qi)
    o = jnp.concatenate([_normalize(c) for c in state], axis=0)
    o_ref[...] = (o.T * gate_ref[...].astype(F32)).astype(BF16)


def _moba(qm, km, vm, kmean, gm, B, S):
    nq = S // TILE
    hps = MOBA_HPS
    wide = hps * HEAD_DIM
    return pl.pallas_call(
        _moba_kernel,
        grid=(B, MOBA_HEADS // hps, nq),
        in_specs=[
            pl.BlockSpec((1, hps, TILE, LANES), lambda b, p, i: (b, p, i, 0)),
            pl.BlockSpec((1, hps, S, LANES), lambda b, p, i: (b, p, 0, 0)),
            pl.BlockSpec((1, hps, S, LANES), lambda b, p, i: (b, p, 0, 0)),
            pl.BlockSpec((1, hps, 16, LANES), lambda b, p, i: (b, p, 0, 0)),
            pl.BlockSpec((TILE, wide), lambda b, p, i: (b * nq + i, p)),
        ],
        out_specs=pl.BlockSpec((TILE, wide), lambda b, p, i: (b * nq + i, p)),
        out_shape=jax.ShapeDtypeStruct((B * S, MOBA_HEADS * HEAD_DIM), BF16),
        scratch_shapes=[pltpu.VMEM((hps, nq, LANES, TILE), BF16)],
        compiler_params=_cparams(3),
        name="moba_attn",
    )(qm, km, vm, kmean, gm)


def _nsa_kernel(q_ref, ks_ref, vs_ref, kw_ref, vw_ref, kc_ref, vct_ref, bg_ref, gate_ref, o_ref,
                vst_ref, vwt_ref):
    qi = pl.program_id(2)
    n_tiles = ks_ref.shape[2] // TILE
    n_groups = ks_ref.shape[1]
    groups = range(n_groups)
    heads = range(n_groups * NSA_HPG)
    grp = lambda h: h // NSA_HPG

    @pl.when(qi == 0)
    def _():
        for g in groups:
            _fill_transposed(vst_ref.at[g], vs_ref.at[0, g], n_tiles)
            _fill_transposed(vwt_ref.at[g], vw_ref.at[0, g], n_tiles)

    causal, anti = _tile_masks()
    qts = [q_ref[0, h].T for h in heads]

    n_cmp = kc_ref.shape[2]
    t_c = qi * TILE + lax.broadcasted_iota(jnp.int32, (n_cmp, TILE), 1)
    ok_c = t_c >= lax.broadcasted_iota(jnp.int32, (n_cmp, TILE), 0) * CMP_STRIDE + (CMP_LEN - 1)
    s_cmp = [jnp.where(ok_c, _dot(kc_ref[0, grp(h)], qts[h]), NEG) for h in heads]
    r_cmp = []
    for h in heads:
        e = jnp.exp(s_cmp[h] - jnp.max(s_cmp[h], axis=0, keepdims=True))
        p = jnp.where(ok_c, e / jnp.sum(e, axis=0, keepdims=True), 0.0)
        r_cmp.append(_dot(vct_ref[0, grp(h)], p.astype(BF16)))

    n_sel = NSA_POS0 - NSA_SEL0
    n = lax.broadcasted_iota(jnp.int32, (n_sel, TILE), 0)
    blk = (qi * TILE + lax.broadcasted_iota(jnp.int32, (n_sel, TILE), 1)) // SLC_BLOCK
    forced = (n == 0) | (n == blk) | (n == blk - 1)
    qsel = []
    for g in groups:
        r = [r_cmp[g * NSA_HPG + j][NSA_SEL0:NSA_POS0] for j in range(NSA_HPG)]
        imp = jnp.where(forced, FORCE_BONUS, (r[0] + r[1]) + (r[2] + r[3]))
        imp = jnp.where(n <= blk, imp, NEG)
        keep = (_rank_rows(imp, n_sel) < SLC_TOPN) & (n <= blk)
        bias = jnp.where(keep, 0.0, NEG).astype(BF16)
        qsel += [jnp.concatenate([qts[g * NSA_HPG + j][:NSA_SEL0], bias, qts[g * NSA_HPG + j][NSA_POS0:]],
                                 axis=0) for j in range(NSA_HPG)]

    wk = lambda h, j: kw_ref[0, grp(h), _tile_slice(jnp.maximum(j, 0)), :]
    wv = lambda h, j: vwt_ref[grp(h), jnp.maximum(j, 0), :V_ROWS, :]
    sk = lambda h, j: ks_ref[0, grp(h), _tile_slice(j), :]
    sv = lambda h, j: vst_ref[grp(h), j, :V_ROWS, :]
    nh = len(heads)
    items = (_tile_items(qts, wk, wv, qi, causal) + _tile_items(qts, wk, wv, qi - 1, qi >= 1)
             + _tile_items(qts, wk, wv, qi - 2, jnp.logical_and(anti, qi >= 2))
             + _tile_items(qsel, sk, sv, qi, causal, slot0=nh))
    state = _attend(items, _init_state(2 * nh))
    win, sel = state[:nh], _past_sweep(state[nh:], qsel, sk, sv, qi)

    bgs = [bg_ref[0, g].T for g in groups]
    outs = []
    for h in heads:
        bg, r0 = bgs[grp(h)], 3 * (h % NSA_HPG)
        outs.append(bg[r0:r0 + 1] * r_cmp[h][:HEAD_DIM] + bg[r0 + 1:r0 + 2] * _normalize(sel[h])
                    + bg[r0 + 2:r0 + 3] * _normalize(win[h]))
    o = jnp.concatenate(outs, axis=0)
    o_ref[...] = (o.T * gate_ref[...].astype(F32)).astype(BF16)


NSA_GPS = 2


def _nsa(qn, ks, vs, kw, vw, kcmp, vcmp_t, gates, gn, B, S):
    nq = S // TILE
    nrow = S // CMP_STRIDE
    gps = NSA_GPS
    kv_spec = pl.BlockSpec((1, gps, S, LANES), lambda b, g, i: (b, g, 0, 0))
    cmp_spec = pl.BlockSpec((1, gps, nrow, LANES), lambda b, g, i: (b, g, 0, 0))
    wide = gps * NSA_HPG * HEAD_DIM
    return pl.pallas_call(
        _nsa_kernel,
        grid=(B, NSA_KV_GROUPS // gps, nq),
        in_specs=[
            pl.BlockSpec((1, gps * NSA_HPG, TILE, LANES), lambda b, g, i: (b, g, i, 0)),
            kv_spec, kv_spec, kv_spec, kv_spec, cmp_spec, cmp_spec,
            pl.BlockSpec((1, gps, TILE, LANES), lambda b, g, i: (b, g, i, 0)),
            pl.BlockSpec((TILE, wide), lambda b, g, i: (b * nq + i, g)),
        ],
        out_specs=pl.BlockSpec((TILE, wide), lambda b, g, i: (b * nq + i, g)),
        out_shape=jax.ShapeDtypeStruct((B * S, NSA_HEADS * HEAD_DIM), BF16),
        scratch_shapes=[pltpu.VMEM((gps, nq, LANES, TILE), BF16), pltpu.VMEM((gps, nq, LANES, TILE), BF16)],
        compiler_params=_cparams(3),
        name="nsa_attn",
    )(qn, ks, vs, kw, vw, kcmp, vcmp_t, gates, gn)


def _outproj_kernel(ym_ref, yn_ref, w_ref, x_ref, mod_ref, g_ref, o_ref):
    D = x_ref.shape[1]
    half = ym_ref.shape[1]
    z = _dot(ym_ref[...], w_ref[:half, :]) + _dot(yn_ref[...], w_ref[half:, :])
    r = z * lax.rsqrt(jnp.mean(z * z, axis=-1, keepdims=True) + RMS_EPS) * g_ref[...]
    o_ref[...] = x_ref[...] + mod_ref[0][:, 2 * D:] * r


def _outproj(ym, yn, w_out, x2, mod, g_post, S):
    M, D = x2.shape
    tm = ROW_TILE
    nt = S // tm
    row_spec = lambda w: pl.BlockSpec((tm, w), lambda i: (i, 0))
    return pl.pallas_call(
        _outproj_kernel,
        grid=(M // tm,),
        in_specs=[
            row_spec(ym.shape[1]), row_spec(yn.shape[1]),
            pl.BlockSpec(w_out.shape, lambda i: (0, 0)),
            row_spec(D),
            pl.BlockSpec((1, 1, 3 * D), lambda i: (i // nt, 0, 0)),
            pl.BlockSpec((1, D), lambda i: (0, 0)),
        ],
        out_specs=row_spec(D),
        out_shape=jax.ShapeDtypeStruct((M, D), F32),
        compiler_params=_cparams(1),
        name="out_proj",
    )(ym, yn, w_out, x2, mod, g_post)


def kernel(x, c, w_ada, b_ada, g_pre, g_post, w_in, w_out, pe_k, pe_v, w_ck1, w_ck2, w_cv1, w_cv2):
    B, S, D = x.shape
    L = w_ada.shape[0]
    assert D == (MOBA_HEADS + NSA_HEADS) * HEAD_DIM and S % ROW_TILE == 0
    assert S // MOBA_BLOCK <= 8 and S // SLC_BLOCK <= 32 and S // CMP_STRIDE <= LANES
    nb = S // MOBA_BLOCK

    mod_all = _modulation(c, w_ada, b_ada)
    x2 = x.reshape(B * S, D)
    for l in range(L):
        mod = mod_all[l].reshape(B, 1, 3 * D)
        (qm, km, vm, kmean, gm, qn, ks, vs, kw, vw, kc, vc, gates, gn) = _inproj(
            x2, mod, g_pre[l].reshape(1, D), _pack_w_in(w_in[l]), B, S)
        kmean = kmean.reshape(B, nb, MOBA_HEADS, HEAD_DIM).transpose(0, 2, 1, 3)
        kmean = jnp.pad(kmean, ((0, 0), (0, 0), (0, 16 - nb), (0, LANES - HEAD_DIM))).astype(BF16)
        kcmp, vcmp = _compress(kc, vc, _pack_compress_weights(pe_k[l], w_ck1[l], w_ck2[l]),
                               _pack_compress_weights(pe_v[l], w_cv1[l], w_cv2[l]), B, S)
        ym = _moba(qm, km, vm, kmean, gm, B, S)
        yn = _nsa(qn, ks, vs, kw, vw, kcmp, vcmp, gates, gn, B, S)
        x2 = _outproj(ym, yn, w_out[l].astype(BF16), x2, mod, g_post[l].reshape(1, D), S)
    return x2.reshape(B, S, D)
```

```python
import functools

import numpy as np
import jax
import jax.numpy as jnp
from jax import lax
from jax.experimental import pallas as pl
from jax.experimental.pallas import tpu as pltpu

HEAD_DIM = 64
MOBA_HEADS = 8
NSA_HEADS = 8
NSA_KV_GROUPS = 2
NSA_HPG = NSA_HEADS // NSA_KV_GROUPS
MOBA_BLOCK = 256
MOBA_TOPK = 3
CMP_LEN = 32
CMP_STRIDE = 16
SLC_BLOCK = 64
SLC_TOPN = 16
WINDOW = 512
RMS_EPS = 1e-6
NEG = -1e9
FORCE_BONUS = 1e4

LANES = 128
TILE = 256
POS_RADIX = 16
ROW_TILE = 512
VMEM_LIMIT = 56 * 1024 * 1024

MOBA_SEL0 = HEAD_DIM
MOBA_POS0 = MOBA_SEL0 + 8
NSA_SEL0 = HEAD_DIM
NSA_POS0 = NSA_SEL0 + 32

BF16 = jnp.bfloat16
F32 = jnp.float32


def _slope(i, n):
    return 2.0 ** (-8.0 * (i + 1) / n)


def _dot(a, b):
    return jnp.dot(a, b, preferred_element_type=F32)


def _sigmoid(x):
    return 1.0 / (1.0 + jnp.exp(-x))


def _cparams(n_axes):
    return pltpu.CompilerParams(
        dimension_semantics=("arbitrary",) * n_axes, vmem_limit_bytes=VMEM_LIMIT)


def _mod_kernel(c_ref, w_ref, b_ref, o_ref):
    c = c_ref[...]
    cs = (c * _sigmoid(c)).astype(BF16)
    o_ref[0] = _dot(cs, w_ref[0].astype(BF16)) + b_ref[0]


def _modulation(c, w_ada, b_ada):
    L, D, N = w_ada.shape
    B = c.shape[0]
    tn = 512
    return pl.pallas_call(
        _mod_kernel,
        grid=(L, N // tn),
        in_specs=[
            pl.BlockSpec((B, D), lambda l, j: (0, 0)),
            pl.BlockSpec((1, D, tn), lambda l, j: (l, 0, j)),
            pl.BlockSpec((1, 1, tn), lambda l, j: (l, 0, j)),
        ],
        out_specs=pl.BlockSpec((1, B, tn), lambda l, j: (l, 0, j)),
        out_shape=jax.ShapeDtypeStruct((L, B, N), F32),
        compiler_params=_cparams(2),
        name="adaln_mod",
    )(c, w_ada, b_ada.reshape(L, 1, N))


_SEG = {}
_off = 0
for _name, _w in (("qm", 512), ("km", 512), ("vm", 512), ("zm", 512), ("qn", 512),
                  ("kc", 128), ("vc", 128), ("ks", 128), ("vs", 128), ("kw", 128),
                  ("vw", 128), ("zn", 512), ("gl", 128)):
    _SEG[_name] = (_off, _w)
    _off += _w
D_IN_PACKED = _off


def _pack_w_in(w_in):
    offs = np.cumsum([0, 512, 512, 512, 512, 512, 128, 128, 128, 128, 128, 128, 24, 512])
    parts = [w_in[:, offs[i]:offs[i + 1]] for i in range(13)]
    gl, zn = parts[11], parts[12]
    gl = jnp.pad(gl, ((0, 0), (0, LANES - gl.shape[1])))
    return jnp.concatenate(parts[:11] + [zn, gl], axis=1).astype(BF16)


def _inproj_kernel(x_ref, mod_ref, g_ref, w_ref,
                   qm_ref, km_ref, vm_ref, kmean_ref, gm_ref,
                   qn_ref, ks_ref, vs_ref, kw_ref, vw_ref,
                   kc_ref, vc_ref, gate_ref, gn_ref, *, seq_len):
    tm, D = x_ref.shape
    x = x_ref[...]
    mod = mod_ref[0]
    shift, scale = mod[:, :D], mod[:, D:2 * D]
    y = x * lax.rsqrt(jnp.mean(x * x, axis=-1, keepdims=True) + RMS_EPS) * g_ref[...]
    hb = (y * (1.0 + scale) + shift).astype(BF16)

    def seg(name):
        o, w = _SEG[name]
        return _dot(hb, w_ref[:, o:o + w])

    lane = lax.broadcasted_iota(jnp.int32, (tm, LANES), 1)
    row = lax.broadcasted_iota(jnp.int32, (tm, LANES), 0)
    t = (pl.program_id(0) % (seq_len // tm)) * tm + row
    t_hi = (t // POS_RADIX).astype(F32)
    t_lo = (t % POS_RADIX).astype(F32)

    def pos_lanes(p0, v0, v1, v2, v3):
        z = jnp.zeros((tm, LANES), F32)
        return jnp.where(lane == p0, v0, jnp.where(lane == p0 + 1, v1,
                         jnp.where(lane == p0 + 2, v2, jnp.where(lane == p0 + 3, v3, z))))

    one = jnp.ones((tm, LANES), F32)
    q_pos_m = pos_lanes(MOBA_POS0, POS_RADIX * one, one, -POS_RADIX * t_hi, -t_lo)
    k_pos_m = pos_lanes(MOBA_POS0, t_hi, t_lo, one, one)
    q_pos_n = pos_lanes(NSA_POS0, POS_RADIX * one, one, -POS_RADIX * t_hi, -t_lo)
    k_pos_n = pos_lanes(NSA_POS0, t_hi, t_lo, one, one)
    k_ext_m = k_pos_m + jnp.where(lane - MOBA_SEL0 == t // MOBA_BLOCK, 1.0, 0.0)
    k_ext_s = k_pos_n + jnp.where(lane - NSA_SEL0 == t // SLC_BLOCK, 1.0, 0.0)
    v_ext = jnp.where(lane == HEAD_DIM, 1.0, 0.0)
    low = lane < HEAD_DIM

    def head(u, h, ext):
        p = u[:, (h // 2) * LANES:(h // 2 + 1) * LANES]
        if h % 2:
            p = pltpu.roll(p, HEAD_DIM, axis=1)
        return jnp.where(low, p, ext).astype(BF16)

    u = seg("qm") * (HEAD_DIM ** -0.5)
    for h in range(MOBA_HEADS):
        qm_ref[0, h] = head(u, h, _slope(h, MOBA_HEADS) * q_pos_m)
    u = seg("km")
    for h in range(MOBA_HEADS):
        km_ref[0, h] = head(u, h, k_ext_m)
    for r in range(tm // MOBA_BLOCK):
        kmean_ref[0, r] = jnp.mean(u[r * MOBA_BLOCK:(r + 1) * MOBA_BLOCK], axis=0, keepdims=True)
    u = seg("vm")
    for h in range(MOBA_HEADS):
        vm_ref[0, h] = head(u, h, v_ext)
    u = seg("zm")
    gm_ref[...] = (u * _sigmoid(u)).astype(BF16)
    u = seg("qn") * (HEAD_DIM ** -0.5)
    for h in range(NSA_HEADS):
        qn_ref[0, h] = head(u, h, _slope(h, NSA_HEADS) * q_pos_n)
    for name, ref, ext in (("ks", ks_ref, k_ext_s), ("vs", vs_ref, v_ext),
                           ("kw", kw_ref, k_pos_n), ("vw", vw_ref, v_ext)):
        u = seg(name)
        for g in range(NSA_KV_GROUPS):
            ref[0, g] = head(u, g, ext)
    kc_ref[...] = seg("kc")
    vc_ref[...] = seg("vc")
    u = seg("zn")
    gn_ref[...] = (u * _sigmoid(u)).astype(BF16)
    sg = _sigmoid(seg("gl"))
    for g in range(NSA_KV_GROUPS):
        gate_ref[0, g] = sg if g == 0 else pltpu.roll(sg, LANES - g * 3 * NSA_HPG, axis=1)


def _inproj(x2, mod, g_pre, w_packed, B, S):
    M, D = x2.shape
    tm = ROW_TILE
    nt = S // tm
    grid = (M // tm,)
    bs = lambda i: i // nt
    si = lambda i: i % nt
    head_spec = lambda nh: pl.BlockSpec((1, nh, tm, LANES), lambda i: (bs(i), 0, si(i), 0))
    row_spec = lambda w: pl.BlockSpec((tm, w), lambda i: (i, 0))
    sds = jax.ShapeDtypeStruct
    out_shape = (
        sds((B, MOBA_HEADS, S, LANES), BF16),
        sds((B, MOBA_HEADS, S, LANES), BF16),
        sds((B, MOBA_HEADS, S, LANES), BF16),
        sds((B, S // MOBA_BLOCK, 1, 512), F32),
        sds((M, 512), BF16),
        sds((B, NSA_HEADS, S, LANES), BF16),
        sds((B, NSA_KV_GROUPS, S, LANES), BF16),
        sds((B, NSA_KV_GROUPS, S, LANES), BF16),
        sds((B, NSA_KV_GROUPS, S, LANES), BF16),
        sds((B, NSA_KV_GROUPS, S, LANES), BF16),
        sds((M, LANES), F32),
        sds((M, LANES), F32),
        sds((B, NSA_KV_GROUPS, S, LANES), F32),
        sds((M, 512), BF16),
    )
    out_specs = (
        head_spec(MOBA_HEADS), head_spec(MOBA_HEADS), head_spec(MOBA_HEADS),
        pl.BlockSpec((1, tm // MOBA_BLOCK, 1, 512), lambda i: (bs(i), si(i), 0, 0)),
        row_spec(512),
        head_spec(NSA_HEADS),
        head_spec(NSA_KV_GROUPS), head_spec(NSA_KV_GROUPS),
        head_spec(NSA_KV_GROUPS), head_spec(NSA_KV_GROUPS),
        row_spec(LANES), row_spec(LANES),
        head_spec(NSA_KV_GROUPS),
        row_spec(512),
    )
    return pl.pallas_call(
        functools.partial(_inproj_kernel, seq_len=S),
        grid=grid,
        in_specs=[
            row_spec(D),
            pl.BlockSpec((1, 1, 3 * D), lambda i: (bs(i), 0, 0)),
            pl.BlockSpec((1, D), lambda i: (0, 0)),
            pl.BlockSpec((D, D_IN_PACKED), lambda i: (0, 0)),
        ],
        out_specs=out_specs,
        out_shape=out_shape,
        compiler_params=_cparams(1),
        name="in_proj",
    )(x2, mod, g_pre, w_packed)


def _gelu_tanh(x):
    return 0.5 * x * (1.0 + jnp.tanh(np.sqrt(2.0 / np.pi) * (x + 0.044715 * (x * x * x))))


def _compress_kernel(kc_ref, vc_ref, pek_ref, pev_ref, wk1_ref, wk2_ref, wv1_ref, wv2_ref,
                     m_ref, kout_ref, vout_ref, win_ref):
    S = kc_ref.shape[0]
    nrow = S // CMP_STRIDE
    lane = lax.broadcasted_iota(jnp.int32, (nrow, LANES), 1)
    row = lax.broadcasted_iota(jnp.int32, (nrow, LANES), 0)
    low = lane < HEAD_DIM
    valid = row < nrow - 1
    k_ext = jnp.where(lane == NSA_POS0, (row + 1).astype(F32),
                      jnp.where(lane == NSA_POS0 + 1, float(POS_RADIX - 1),
                                jnp.where((lane == NSA_POS0 + 2) | (lane == NSA_POS0 + 3), 1.0, 0.0)))

    def run(src_ref, pe_ref, w1_ref, w2_ref, out_ref, ext, transposed):
        for l in range(CMP_STRIDE):
            win_ref[:, l * LANES:(l + 1) * LANES] = src_ref[pl.ds(l, nrow, stride=CMP_STRIDE), :]
        xw = win_ref[...]
        lo = _dot((xw + pe_ref[0:1, :]).astype(BF16), w1_ref[0])
        hi = _dot((xw + pe_ref[1:2, :]).astype(BF16), w1_ref[1])
        pre = lo + pltpu.roll(hi, nrow - 1, axis=0)
        out = _dot(_gelu_tanh(pre).astype(BF16), w2_ref[...])
        out = jnp.where(valid, out, 0.0)
        for g in range(NSA_KV_GROUPS):
            p = out if g == 0 else pltpu.roll(out, HEAD_DIM, axis=1)
            p = jnp.where(low, p, ext)
            out_ref[0, g] = (p.T if transposed else p).astype(BF16)

    run(kc_ref, pek_ref, wk1_ref, wk2_ref, kout_ref, k_ext, False)
    run(vc_ref, pev_ref, wv1_ref, wv2_ref, vout_ref, jnp.where(valid, m_ref[...], 0.0), True)


def _pack_compress_weights(pe, w1, w2):
    half = CMP_STRIDE
    eye = jnp.eye(NSA_KV_GROUPS, dtype=F32)
    w1r = w1.reshape(2, half, HEAD_DIM, HEAD_DIM)
    w1bd = jnp.einsum("hlde,gk->hlgdke", w1r, eye).reshape(2, half * LANES, LANES).astype(BF16)
    w2bd = jnp.einsum("de,gk->gdke", w2, eye).reshape(LANES, LANES).astype(BF16)
    pet = jnp.tile(pe.reshape(2, half, 1, HEAD_DIM), (1, 1, NSA_KV_GROUPS, 1)).reshape(2, half * LANES)
    return pet, w1bd, w2bd


def _cmp_to_slc_lanes(n_rows, n_slc):
    i = np.arange(n_rows)[:, None]
    j = np.arange(n_slc)[None, :]
    start = i * CMP_STRIDE
    ov = (start < (j + 1) * SLC_BLOCK) & (start + CMP_LEN > j * SLC_BLOCK)
    m = np.zeros((n_rows, LANES), np.float32)
    m[:, NSA_SEL0:NSA_SEL0 + n_slc] = ov
    return jnp.asarray(m)


def _compress(kc, vc, pk, pv, B, S):
    nrow = S // CMP_STRIDE
    pek, wk1, wk2 = pk
    pev, wv1, wv2 = pv
    m = _cmp_to_slc_lanes(nrow, S // SLC_BLOCK)
    full = lambda a: pl.BlockSpec(a.shape, lambda b: (0,) * a.ndim)
    out_spec = pl.BlockSpec((1, NSA_KV_GROUPS, nrow, LANES), lambda b: (b, 0, 0, 0))
    out_sds = jax.ShapeDtypeStruct((B, NSA_KV_GROUPS, nrow, LANES), BF16)
    return pl.pallas_call(
        _compress_kernel,
        grid=(B,),
        in_specs=[pl.BlockSpec((S, LANES), lambda b: (b, 0)), pl.BlockSpec((S, LANES), lambda b: (b, 0)),
                  full(pek), full(pev), full(wk1), full(wk2), full(wv1), full(wv2), full(m)],
        out_specs=(out_spec, out_spec),
        out_shape=(out_sds, out_sds),
        scratch_shapes=[pltpu.VMEM((nrow, CMP_STRIDE * LANES), F32)],
        compiler_params=_cparams(1),
        name="nsa_compress",
    )(kc, vc, pek, pev, wk1, wk2, wv1, wv2, m)


def _rank_rows(vals, count):
    idx = lax.broadcasted_iota(jnp.int32, vals.shape, 0)
    rank = jnp.zeros(vals.shape, jnp.int32)
    for j in range(count):
        rowj = vals[j:j + 1, :]
        beats = (rowj > vals) | ((rowj == vals) & (j < idx))
        rank = rank + beats.astype(jnp.int32)
    return rank


def _tile_masks():
    kpos = lax.broadcasted_iota(jnp.int32, (TILE, TILE), 0)
    qpos = lax.broadcasted_iota(jnp.int32, (TILE, TILE), 1)
    return kpos <= qpos, kpos > qpos


def _tile_slice(j):
    return pl.ds(pl.multiple_of(j * TILE, TILE), TILE)


HALF = TILE // 2
V_ROWS = 80
QK_LEAD = 5
ACC_LAG = 2


def _attend(items, m_ref, acc_ref, fresh):
    n = len(items)
    m_val, acc_val = {}, {}
    scores, updates = {}, {}
    for t in range(n + QK_LEAD + ACC_LAG):
        if t < n:
            _, qt, k, _, mask = items[t]
            s = _dot(k, qt)
            scores[t] = s if mask is None else jnp.where(mask, s, NEG)
        i = t - QK_LEAD
        if 0 <= i < n:
            slot, _, _, vt, _ = items[i]
            s = scores.pop(i)
            if slot not in m_val and not fresh:
                m_val[slot] = m_ref[slot]
            m = m_val.get(slot)
            c0 = jnp.max(s[:HALF], axis=0, keepdims=True)
            m0 = c0 if m is None else jnp.maximum(m, c0)
            p0 = jnp.exp(s[:HALF] - m0).astype(BF16)
            m1 = jnp.maximum(m0, jnp.max(s[HALF:], axis=0, keepdims=True))
            p1 = jnp.exp(s[HALF:] - m1).astype(BF16)
            p0 = p0 * jnp.exp(m0 - m1).astype(BF16)
            updates[i] = (None if m is None else jnp.exp(m - m1),
                          _dot(vt, jnp.concatenate([p0, p1], axis=0)))
            m_val[slot] = m1
        i = t - QK_LEAD - ACC_LAG
        if 0 <= i < n:
            slot = items[i][0]
            alpha, pv = updates.pop(i)
            if alpha is None:
                acc_val[slot] = pv
            else:
                if slot not in acc_val:
                    acc_val[slot] = acc_ref[slot]
                acc_val[slot] = alpha * acc_val[slot] + pv
    for slot in m_val:
        m_ref[slot] = m_val[slot]
        acc_ref[slot] = acc_val[slot]


def _tile_items(qts, k_of, vt_of, j, mask=None, slot0=0):
    return [(slot0 + h, qt, k_of(h, j), vt_of(h, j), mask) for h, qt in enumerate(qts)]


PAST_GROUP = 4


def _causal_sweep(m_ref, acc_ref, other_items, qts, k_of, vt_of, qi, causal, slot0=0):
    def tiles(js):
        return sum((_tile_items(qts, k_of, vt_of, j, slot0=slot0) for j in js), [])

    def head_block(r):
        def run():
            _attend(other_items() + _tile_items(qts, k_of, vt_of, qi, causal, slot0)
                    + tiles([qi - r + i for i in range(r)]), m_ref, acc_ref, True)
            return jnp.int32(0)
        return run

    def trip(i, carry):
        _attend(tiles([PAST_GROUP * i + c for c in range(PAST_GROUP)]), m_ref, acc_ref, False)
        return carry

    lax.switch(qi % PAST_GROUP, [head_block(r) for r in range(PAST_GROUP)])
    lax.fori_loop(0, qi // PAST_GROUP, trip, jnp.int32(0))


def _normalize(acc):
    return acc[:HEAD_DIM] * (1.0 / acc[HEAD_DIM:HEAD_DIM + 1])


def _fill_transposed(vt_ref, v_ref, n_tiles):
    for c in range(n_tiles):
        vt_ref[c] = v_ref[c * TILE:(c + 1) * TILE, :].T


MOBA_HPS = 8


def _moba_kernel(q_ref, k_ref, v_ref, kmean_ref, gate_ref, o_ref, vt_ref, m_ref, acc_ref):
    qi = pl.program_id(2)
    n_tiles = k_ref.shape[2] // TILE
    heads = range(q_ref.shape[1])

    @pl.when(qi == 0)
    def _():
        for h in heads:
            _fill_transposed(vt_ref.at[h], v_ref.at[0, h], n_tiles)

    blk = lax.broadcasted_iota(jnp.int32, (16, TILE), 0)
    causal, _ = _tile_masks()
    qts = [q_ref[0, h].T for h in heads]
    gscs = [jnp.where(blk < qi, _dot(kmean_ref[0, h], qts[h]), NEG) for h in heads]
    qsel = []
    for h in heads:
        keep = ((blk < qi) & (_rank_rows(gscs[h], 8) < jnp.minimum(MOBA_TOPK, qi))) | (blk == qi)
        ext = qts[h][MOBA_SEL0:MOBA_SEL0 + 16].astype(F32)
        ext = jnp.where((blk < 8) & jnp.logical_not(keep), NEG, ext).astype(BF16)
        qsel.append(jnp.concatenate([qts[h][:MOBA_SEL0], ext, qts[h][MOBA_SEL0 + 16:]], axis=0))
    k_of = lambda h, j: k_ref[0, h, _tile_slice(j), :]
    vt_of = lambda h, j: vt_ref[h, j, :V_ROWS, :]
    _causal_sweep(m_ref, acc_ref, lambda: [], qsel, k_of, vt_of, qi, causal)
    o = jnp.concatenate([_normalize(acc_ref[h]) for h in heads], axis=0)
    o_ref[...] = (o.T * gate_ref[...].astype(F32)).astype(BF16)


def _moba(qm, km, vm, kmean, gm, B, S):
    nq = S // TILE
    hps = MOBA_HPS
    wide = hps * HEAD_DIM
    return pl.pallas_call(
        _moba_kernel,
        grid=(B, MOBA_HEADS // hps, nq),
        in_specs=[
            pl.BlockSpec((1, hps, TILE, LANES), lambda b, p, i: (b, p, i, 0)),
            pl.BlockSpec((1, hps, S, LANES), lambda b, p, i: (b, p, 0, 0)),
            pl.BlockSpec((1, hps, S, LANES), lambda b, p, i: (b, p, 0, 0)),
            pl.BlockSpec((1, hps, 16, LANES), lambda b, p, i: (b, p, 0, 0)),
            pl.BlockSpec((TILE, wide), lambda b, p, i: (b * nq + i, p)),
        ],
        out_specs=pl.BlockSpec((TILE, wide), lambda b, p, i: (b * nq + i, p)),
        out_shape=jax.ShapeDtypeStruct((B * S, MOBA_HEADS * HEAD_DIM), BF16),
        scratch_shapes=[pltpu.VMEM((hps, nq, LANES, TILE), BF16),
                        pltpu.VMEM((hps, 1, TILE), F32), pltpu.VMEM((hps, V_ROWS, TILE), F32)],
        compiler_params=_cparams(3),
        name="moba_attn",
    )(qm, km, vm, kmean, gm)


def _nsa_kernel(q_ref, ks_ref, vs_ref, kw_ref, vw_ref, kc_ref, vct_ref, bg_ref, gate_ref, o_ref,
                vst_ref, vwt_ref, m_ref, acc_ref):
    qi = pl.program_id(2)
    n_tiles = ks_ref.shape[2] // TILE
    n_groups = ks_ref.shape[1]
    groups = range(n_groups)
    heads = range(n_groups * NSA_HPG)
    grp = lambda h: h // NSA_HPG

    @pl.when(qi == 0)
    def _():
        for g in groups:
            _fill_transposed(vst_ref.at[g], vs_ref.at[0, g], n_tiles)
            _fill_transposed(vwt_ref.at[g], vw_ref.at[0, g], n_tiles)

    causal, anti = _tile_masks()
    qts = [q_ref[0, h].T for h in heads]

    n_cmp = kc_ref.shape[2]
    t_c = qi * TILE + lax.broadcasted_iota(jnp.int32, (n_cmp, TILE), 1)
    ok_c = t_c >= lax.broadcasted_iota(jnp.int32, (n_cmp, TILE), 0) * CMP_STRIDE + (CMP_LEN - 1)
    s_cmp = [jnp.where(ok_c, _dot(kc_ref[0, grp(h)], qts[h]), NEG) for h in heads]
    r_cmp = []
    for h in heads:
        e = jnp.exp(s_cmp[h] - jnp.max(s_cmp[h], axis=0, keepdims=True))
        p = jnp.where(ok_c, e * (1.0 / jnp.sum(e, axis=0, keepdims=True)), 0.0)
        r_cmp.append(_dot(vct_ref[0, grp(h), :NSA_POS0, :], p.astype(BF16)))

    n_sel = NSA_POS0 - NSA_SEL0
    n = lax.broadcasted_iota(jnp.int32, (n_sel, TILE), 0)
    blk = (qi * TILE + lax.broadcasted_iota(jnp.int32, (n_sel, TILE), 1)) // SLC_BLOCK
    forced = (n == 0) | (n == blk) | (n == blk - 1)
    qsel = []
    for g in groups:
        r = [r_cmp[g * NSA_HPG + j][NSA_SEL0:NSA_POS0] for j in range(NSA_HPG)]
        imp = jnp.where(forced, FORCE_BONUS, (r[0] + r[1]) + (r[2] + r[3]))
        imp = jnp.where(n <= blk, imp, NEG)
        bias = lax.cond(
            (qi + 1) * TILE > SLC_TOPN * SLC_BLOCK,
            lambda v: jnp.where((_rank_rows(v, n_sel) < SLC_TOPN) & (n <= blk), 0.0, NEG),
            lambda v: jnp.where(n <= blk, 0.0, NEG), imp).astype(BF16)
        qsel += [jnp.concatenate([qts[g * NSA_HPG + j][:NSA_SEL0], bias, qts[g * NSA_HPG + j][NSA_POS0:]],
                                 axis=0) for j in range(NSA_HPG)]

    wk = lambda h, j: kw_ref[0, grp(h), _tile_slice(jnp.maximum(j, 0)), :]
    wv = lambda h, j: vwt_ref[grp(h), jnp.maximum(j, 0), :V_ROWS, :]
    sk = lambda h, j: ks_ref[0, grp(h), _tile_slice(j), :]
    sv = lambda h, j: vst_ref[grp(h), j, :V_ROWS, :]
    nh = len(heads)
    window_items = lambda: (_tile_items(qts, wk, wv, qi, causal) + _tile_items(qts, wk, wv, qi - 1, qi >= 1)
                            + _tile_items(qts, wk, wv, qi - 2, jnp.logical_and(anti, qi >= 2)))
    _causal_sweep(m_ref, acc_ref, window_items, qsel, sk, sv, qi, causal, slot0=nh)

    bgs = [bg_ref[0, g].T for g in groups]
    outs = []
    for h in heads:
        bg, r0 = bgs[grp(h)], 3 * (h % NSA_HPG)
        outs.append(bg[r0:r0 + 1] * r_cmp[h][:HEAD_DIM] + bg[r0 + 1:r0 + 2] * _normalize(acc_ref[nh + h])
                    + bg[r0 + 2:r0 + 3] * _normalize(acc_ref[h]))
    o = jnp.concatenate(outs, axis=0)
    o_ref[...] = (o.T * gate_ref[...].astype(F32)).astype(BF16)


NSA_GPS = 2


def _nsa(qn, ks, vs, kw, vw, kcmp, vcmp_t, gates, gn, B, S):
    nq = S // TILE
    nrow = S // CMP_STRIDE
    gps = NSA_GPS
    kv_spec = pl.BlockSpec((1, gps, S, LANES), lambda b, g, i: (b, g, 0, 0))
    cmp_spec = pl.BlockSpec((1, gps, nrow, LANES), lambda b, g, i: (b, g, 0, 0))
    wide = gps * NSA_HPG * HEAD_DIM
    return pl.pallas_call(
        _nsa_kernel,
        grid=(B, NSA_KV_GROUPS // gps, nq),
        in_specs=[
            pl.BlockSpec((1, gps * NSA_HPG, TILE, LANES), lambda b, g, i: (b, g, i, 0)),
            kv_spec, kv_spec, kv_spec, kv_spec, cmp_spec, cmp_spec,
            pl.BlockSpec((1, gps, TILE, LANES), lambda b, g, i: (b, g, i, 0)),
            pl.BlockSpec((TILE, wide), lambda b, g, i: (b * nq + i, g)),
        ],
        out_specs=pl.BlockSpec((TILE, wide), lambda b, g, i: (b * nq + i, g)),
        out_shape=jax.ShapeDtypeStruct((B * S, NSA_HEADS * HEAD_DIM), BF16),
        scratch_shapes=[pltpu.VMEM((gps, nq, LANES, TILE), BF16), pltpu.VMEM((gps, nq, LANES, TILE), BF16),
                        pltpu.VMEM((2 * gps * NSA_HPG, 1, TILE), F32),
                        pltpu.VMEM((2 * gps * NSA_HPG, V_ROWS, TILE), F32)],
        compiler_params=_cparams(3),
        name="nsa_attn",
    )(qn, ks, vs, kw, vw, kcmp, vcmp_t, gates, gn)


def _outproj_kernel(ym_ref, yn_ref, w_ref, x_ref, mod_ref, g_ref, o_ref):
    D = x_ref.shape[1]
    half = ym_ref.shape[1]
    z = _dot(ym_ref[...], w_ref[:half, :]) + _dot(yn_ref[...], w_ref[half:, :])
    r = z * lax.rsqrt(jnp.mean(z * z, axis=-1, keepdims=True) + RMS_EPS) * g_ref[...]
    o_ref[...] = x_ref[...] + mod_ref[0][:, 2 * D:] * r


def _outproj(ym, yn, w_out, x2, mod, g_post, S):
    M, D = x2.shape
    tm = ROW_TILE
    nt = S // tm
    row_spec = lambda w: pl.BlockSpec((tm, w), lambda i: (i, 0))
    return pl.pallas_call(
        _outproj_kernel,
        grid=(M // tm,),
        in_specs=[
            row_spec(ym.shape[1]), row_spec(yn.shape[1]),
            pl.BlockSpec(w_out.shape, lambda i: (0, 0)),
            row_spec(D),
            pl.BlockSpec((1, 1, 3 * D), lambda i: (i // nt, 0, 0)),
            pl.BlockSpec((1, D), lambda i: (0, 0)),
        ],
        out_specs=row_spec(D),
        out_shape=jax.ShapeDtypeStruct((M, D), F32),
        compiler_params=_cparams(1),
        name="out_proj",
    )(ym, yn, w_out, x2, mod, g_post)


def kernel(x, c, w_ada, b_ada, g_pre, g_post, w_in, w_out, pe_k, pe_v, w_ck1, w_ck2, w_cv1, w_cv2):
    B, S, D = x.shape
    L = w_ada.shape[0]
    assert D == (MOBA_HEADS + NSA_HEADS) * HEAD_DIM and S % ROW_TILE == 0
    assert S // MOBA_BLOCK <= 8 and S // SLC_BLOCK <= 32 and S // CMP_STRIDE <= LANES
    nb = S // MOBA_BLOCK

    mod_all = _modulation(c, w_ada, b_ada)
    x2 = x.reshape(B * S, D)
    for l in range(L):
        mod = mod_all[l].reshape(B, 1, 3 * D)
        (qm, km, vm, kmean, gm, qn, ks, vs, kw, vw, kc, vc, gates, gn) = _inproj(
            x2, mod, g_pre[l].reshape(1, D), _pack_w_in(w_in[l]), B, S)
        kmean = kmean.reshape(B, nb, MOBA_HEADS, HEAD_DIM).transpose(0, 2, 1, 3)
        kmean = jnp.pad(kmean, ((0, 0), (0, 0), (0, 16 - nb), (0, LANES - HEAD_DIM))).astype(BF16)
        kcmp, vcmp = _compress(kc, vc, _pack_compress_weights(pe_k[l], w_ck1[l], w_ck2[l]),
                               _pack_compress_weights(pe_v[l], w_cv1[l], w_cv2[l]), B, S)
        ym = _moba(qm, km, vm, kmean, gm, B, S)
        yn = _nsa(qn, ks, vs, kw, vw, kcmp, vcmp, gates, gn, B, S)
        x2 = _outproj(ym, yn, w_out[l].astype(BF16), x2, mod, g_post[l].reshape(1, D), S)
    return x2.reshape(B, S, D)
```

```python
import functools

import numpy as np
import jax
import jax.numpy as jnp
from jax import lax
from jax.experimental import pallas as pl
from jax.experimental.pallas import tpu as pltpu

HEAD_DIM = 64
MOBA_HEADS = 8
NSA_HEADS = 8
NSA_KV_GROUPS = 2
NSA_HPG = NSA_HEADS // NSA_KV_GROUPS
MOBA_BLOCK = 256
MOBA_TOPK = 3
CMP_LEN = 32
CMP_STRIDE = 16
SLC_BLOCK = 64
SLC_TOPN = 16
WINDOW = 512
RMS_EPS = 1e-6
NEG = -1e9
FORCE_BONUS = 1e4

LANES = 128
TILE = 256
POS_RADIX = 16
ROW_TILE = 512
VMEM_LIMIT = 56 * 1024 * 1024

MOBA_SEL0 = HEAD_DIM
MOBA_POS0 = MOBA_SEL0 + 8
NSA_SEL0 = HEAD_DIM
NSA_POS0 = NSA_SEL0 + 32
N_POS = 8
LOG2E = float(np.log2(np.e))

BF16 = jnp.bfloat16
F32 = jnp.float32


def _slope(i, n):
    return 2.0 ** (-8.0 * (i + 1) / n)


def _alibi_coef():
    table = np.zeros((MOBA_HEADS + NSA_HEADS, LANES), np.float32)
    for row in range(MOBA_HEADS + NSA_HEADS):
        moba = row < MOBA_HEADS
        c = (_slope(row, MOBA_HEADS) if moba else _slope(row - MOBA_HEADS, NSA_HEADS)) * LOG2E
        parts, rest = [], c
        for _ in range(3):
            part = float(np.float32(rest).astype(BF16).astype(np.float32))
            parts.append(part)
            rest -= part
        p0 = MOBA_POS0 if moba else NSA_POS0
        table[row, p0:p0 + N_POS] = [parts[0], parts[0], parts[1], parts[1], parts[2], parts[2], c, c]
    return jnp.asarray(table)


def _dot(a, b):
    return jnp.dot(a, b, preferred_element_type=F32)


def _sigmoid(x):
    return 1.0 / (1.0 + jnp.exp(-x))


def _cparams(n_axes):
    return pltpu.CompilerParams(
        dimension_semantics=("arbitrary",) * n_axes, vmem_limit_bytes=VMEM_LIMIT)


def _mod_kernel(c_ref, w_ref, b_ref, o_ref):
    c = c_ref[...]
    cs = (c * _sigmoid(c)).astype(BF16)
    o_ref[0] = _dot(cs, w_ref[0].astype(BF16)) + b_ref[0]


def _modulation(c, w_ada, b_ada):
    L, D, N = w_ada.shape
    B = c.shape[0]
    tn = 512
    return pl.pallas_call(
        _mod_kernel,
        grid=(L, N // tn),
        in_specs=[
            pl.BlockSpec((B, D), lambda l, j: (0, 0)),
            pl.BlockSpec((1, D, tn), lambda l, j: (l, 0, j)),
            pl.BlockSpec((1, 1, tn), lambda l, j: (l, 0, j)),
        ],
        out_specs=pl.BlockSpec((1, B, tn), lambda l, j: (l, 0, j)),
        out_shape=jax.ShapeDtypeStruct((L, B, N), F32),
        compiler_params=_cparams(2),
        name="adaln_mod",
    )(c, w_ada, b_ada.reshape(L, 1, N))


_SEG = {}
_off = 0
for _name, _w in (("qm", 512), ("km", 512), ("vm", 512), ("zm", 512), ("qn", 512),
                  ("kc", 128), ("vc", 128), ("ks", 128), ("vs", 128), ("kw", 128),
                  ("vw", 128), ("zn", 512), ("gl", 128)):
    _SEG[_name] = (_off, _w)
    _off += _w
D_IN_PACKED = _off


def _pack_w_in(w_in):
    offs = np.cumsum([0, 512, 512, 512, 512, 512, 128, 128, 128, 128, 128, 128, 24, 512])
    parts = [w_in[:, offs[i]:offs[i + 1]] for i in range(13)]
    gl, zn = parts[11], parts[12]
    gl = jnp.pad(gl, ((0, 0), (0, LANES - gl.shape[1])))
    return jnp.concatenate(parts[:11] + [zn, gl], axis=1).astype(BF16)


def _inproj_kernel(x_ref, mod_ref, g_ref, w_ref, coef_ref,
                   qm_ref, km_ref, vm_ref, kmean_ref, gm_ref,
                   qn_ref, ks_ref, vs_ref, kw_ref, vw_ref,
                   kc_ref, vc_ref, gate_ref, gn_ref, *, seq_len):
    tm, D = x_ref.shape
    x = x_ref[...]
    mod = mod_ref[0]
    shift, scale = mod[:, :D], mod[:, D:2 * D]
    y = x * lax.rsqrt(jnp.mean(x * x, axis=-1, keepdims=True) + RMS_EPS) * g_ref[...]
    hb = (y * (1.0 + scale) + shift).astype(BF16)

    def seg(name):
        o, w = _SEG[name]
        return _dot(hb, w_ref[:, o:o + w])

    lane = lax.broadcasted_iota(jnp.int32, (tm, LANES), 1)
    row = lax.broadcasted_iota(jnp.int32, (tm, LANES), 0)
    t = (pl.program_id(0) % (seq_len // tm)) * tm + row
    t_hi = (t // POS_RADIX).astype(F32)
    t_lo = (t % POS_RADIX).astype(F32)

    def pos_lanes(p0, vals):
        out = jnp.zeros((tm, LANES), F32)
        for i, v in enumerate(vals):
            out = jnp.where(lane == p0 + i, v, out)
        return out

    q_pat = [float(POS_RADIX), 1.0] * 3 + [-POS_RADIX * t_hi, -t_lo]
    k_pat = [t_hi, t_lo] * 3 + [1.0, 1.0]
    q_pos_m, k_pos_m = pos_lanes(MOBA_POS0, q_pat), pos_lanes(MOBA_POS0, k_pat)
    q_pos_n, k_pos_n = pos_lanes(NSA_POS0, q_pat), pos_lanes(NSA_POS0, k_pat)
    k_ext_m = k_pos_m + jnp.where(lane - MOBA_SEL0 == t // MOBA_BLOCK, 1.0, 0.0)
    k_ext_s = k_pos_n + jnp.where(lane - NSA_SEL0 == t // SLC_BLOCK, 1.0, 0.0)
    v_ext = jnp.where(lane == HEAD_DIM, 1.0, 0.0)
    low = lane < HEAD_DIM

    def head(u, h, ext):
        p = u[:, (h // 2) * LANES:(h // 2 + 1) * LANES]
        if h % 2:
            p = pltpu.roll(p, HEAD_DIM, axis=1)
        return jnp.where(low, p, ext).astype(BF16)

    u = seg("qm") * (HEAD_DIM ** -0.5 * LOG2E)
    for h in range(MOBA_HEADS):
        qm_ref[0, h] = head(u, h, coef_ref[h:h + 1, :] * q_pos_m)
    u = seg("km")
    for h in range(MOBA_HEADS):
        km_ref[0, h] = head(u, h, k_ext_m)
    for r in range(tm // MOBA_BLOCK):
        kmean_ref[0, r] = jnp.mean(u[r * MOBA_BLOCK:(r + 1) * MOBA_BLOCK], axis=0, keepdims=True)
    u = seg("vm")
    for h in range(MOBA_HEADS):
        vm_ref[0, h] = head(u, h, v_ext)
    u = seg("zm")
    gm_ref[...] = (u * _sigmoid(u)).astype(BF16)
    u = seg("qn") * (HEAD_DIM ** -0.5 * LOG2E)
    for h in range(NSA_HEADS):
        qn_ref[0, h] = head(u, h, coef_ref[MOBA_HEADS + h:MOBA_HEADS + h + 1, :] * q_pos_n)
    for name, ref, ext in (("ks", ks_ref, k_ext_s), ("vs", vs_ref, v_ext),
                           ("kw", kw_ref, k_pos_n), ("vw", vw_ref, v_ext)):
        u = seg(name)
        for g in range(NSA_KV_GROUPS):
            ref[0, g] = head(u, g, ext)
    kc_ref[...] = seg("kc")
    vc_ref[...] = seg("vc")
    u = seg("zn")
    gn_ref[...] = (u * _sigmoid(u)).astype(BF16)
    sg = _sigmoid(seg("gl"))
    for g in range(NSA_KV_GROUPS):
        gate_ref[0, g] = sg if g == 0 else pltpu.roll(sg, LANES - g * 3 * NSA_HPG, axis=1)


def _inproj(x2, mod, g_pre, w_packed, B, S):
    M, D = x2.shape
    tm = ROW_TILE
    nt = S // tm
    grid = (M // tm,)
    bs = lambda i: i // nt
    si = lambda i: i % nt
    head_spec = lambda nh: pl.BlockSpec((1, nh, tm, LANES), lambda i: (bs(i), 0, si(i), 0))
    row_spec = lambda w: pl.BlockSpec((tm, w), lambda i: (i, 0))
    sds = jax.ShapeDtypeStruct
    out_shape = (
        sds((B, MOBA_HEADS, S, LANES), BF16),
        sds((B, MOBA_HEADS, S, LANES), BF16),
        sds((B, MOBA_HEADS, S, LANES), BF16),
        sds((B, S // MOBA_BLOCK, 1, 512), F32),
        sds((M, 512), BF16),
        sds((B, NSA_HEADS, S, LANES), BF16),
        sds((B, NSA_KV_GROUPS, S, LANES), BF16),
        sds((B, NSA_KV_GROUPS, S, LANES), BF16),
        sds((B, NSA_KV_GROUPS, S, LANES), BF16),
        sds((B, NSA_KV_GROUPS, S, LANES), BF16),
        sds((M, LANES), F32),
        sds((M, LANES), F32),
        sds((B, NSA_KV_GROUPS, S, LANES), F32),
        sds((M, 512), BF16),
    )
    out_specs = (
        head_spec(MOBA_HEADS), head_spec(MOBA_HEADS), head_spec(MOBA_HEADS),
        pl.BlockSpec((1, tm // MOBA_BLOCK, 1, 512), lambda i: (bs(i), si(i), 0, 0)),
        row_spec(512),
        head_spec(NSA_HEADS),
        head_spec(NSA_KV_GROUPS), head_spec(NSA_KV_GROUPS),
        head_spec(NSA_KV_GROUPS), head_spec(NSA_KV_GROUPS),
        row_spec(LANES), row_spec(LANES),
        head_spec(NSA_KV_GROUPS),
        row_spec(512),
    )
    return pl.pallas_call(
        functools.partial(_inproj_kernel, seq_len=S),
        grid=grid,
        in_specs=[
            row_spec(D),
            pl.BlockSpec((1, 1, 3 * D), lambda i: (bs(i), 0, 0)),
            pl.BlockSpec((1, D), lambda i: (0, 0)),
            pl.BlockSpec((D, D_IN_PACKED), lambda i: (0, 0)),
            pl.BlockSpec((MOBA_HEADS + NSA_HEADS, LANES), lambda i: (0, 0)),
        ],
        out_specs=out_specs,
        out_shape=out_shape,
        compiler_params=_cparams(1),
        name="in_proj",
    )(x2, mod, g_pre, w_packed, _alibi_coef())


def _gelu_tanh(x):
    return 0.5 * x * (1.0 + jnp.tanh(np.sqrt(2.0 / np.pi) * (x + 0.044715 * (x * x * x))))


def _compress_kernel(kc_ref, vc_ref, pek_ref, pev_ref, wk1_ref, wk2_ref, wv1_ref, wv2_ref,
                     m_ref, kout_ref, vout_ref, win_ref):
    S = kc_ref.shape[0]
    nrow = S // CMP_STRIDE
    lane = lax.broadcasted_iota(jnp.int32, (nrow, LANES), 1)
    row = lax.broadcasted_iota(jnp.int32, (nrow, LANES), 0)
    low = lane < HEAD_DIM
    valid = row < nrow - 1
    pos = lane - NSA_POS0
    k_ext = jnp.where((pos >= 0) & (pos < N_POS - 2),
                      jnp.where(pos % 2 == 0, (row + 1).astype(F32), float(POS_RADIX - 1)),
                      jnp.where((pos == N_POS - 2) | (pos == N_POS - 1), 1.0, 0.0))

    def run(src_ref, pe_ref, w1_ref, w2_ref, out_ref, ext, transposed):
        for l in range(CMP_STRIDE):
            win_ref[:, l * LANES:(l + 1) * LANES] = src_ref[pl.ds(l, nrow, stride=CMP_STRIDE), :]
        xw = win_ref[...]
        lo = _dot((xw + pe_ref[0:1, :]).astype(BF16), w1_ref[0])
        hi = _dot((xw + pe_ref[1:2, :]).astype(BF16), w1_ref[1])
        pre = lo + pltpu.roll(hi, nrow - 1, axis=0)
        out = _dot(_gelu_tanh(pre).astype(BF16), w2_ref[...])
        out = jnp.where(valid, out, 0.0)
        for g in range(NSA_KV_GROUPS):
            p = out if g == 0 else pltpu.roll(out, HEAD_DIM, axis=1)
            p = jnp.where(low, p, ext)
            out_ref[0, g] = (p.T if transposed else p).astype(BF16)

    run(kc_ref, pek_ref, wk1_ref, wk2_ref, kout_ref, k_ext, False)
    run(vc_ref, pev_ref, wv1_ref, wv2_ref, vout_ref, jnp.where(valid, m_ref[...], 0.0), True)


def _pack_compress_weights(pe, w1, w2):
    half = CMP_STRIDE
    eye = jnp.eye(NSA_KV_GROUPS, dtype=F32)
    w1r = w1.reshape(2, half, HEAD_DIM, HEAD_DIM)
    w1bd = jnp.einsum("hlde,gk->hlgdke", w1r, eye).reshape(2, half * LANES, LANES).astype(BF16)
    w2bd = jnp.einsum("de,gk->gdke", w2, eye).reshape(LANES, LANES).astype(BF16)
    pet = jnp.tile(pe.reshape(2, half, 1, HEAD_DIM), (1, 1, NSA_KV_GROUPS, 1)).reshape(2, half * LANES)
    return pet, w1bd, w2bd


def _cmp_to_slc_lanes(n_rows, n_slc):
    i = np.arange(n_rows)[:, None]
    j = np.arange(n_slc)[None, :]
    start = i * CMP_STRIDE
    ov = (start < (j + 1) * SLC_BLOCK) & (start + CMP_LEN > j * SLC_BLOCK)
    m = np.zeros((n_rows, LANES), np.float32)
    m[:, NSA_SEL0:NSA_SEL0 + n_slc] = ov
    return jnp.asarray(m)


def _compress(kc, vc, pk, pv, B, S):
    nrow = S // CMP_STRIDE
    pek, wk1, wk2 = pk
    pev, wv1, wv2 = pv
    m = _cmp_to_slc_lanes(nrow, S // SLC_BLOCK)
    full = lambda a: pl.BlockSpec(a.shape, lambda b: (0,) * a.ndim)
    out_spec = pl.BlockSpec((1, NSA_KV_GROUPS, nrow, LANES), lambda b: (b, 0, 0, 0))
    out_sds = jax.ShapeDtypeStruct((B, NSA_KV_GROUPS, nrow, LANES), BF16)
    return pl.pallas_call(
        _compress_kernel,
        grid=(B,),
        in_specs=[pl.BlockSpec((S, LANES), lambda b: (b, 0)), pl.BlockSpec((S, LANES), lambda b: (b, 0)),
                  full(pek), full(pev), full(wk1), full(wk2), full(wv1), full(wv2), full(m)],
        out_specs=(out_spec, out_spec),
        out_shape=(out_sds, out_sds),
        scratch_shapes=[pltpu.VMEM((nrow, CMP_STRIDE * LANES), F32)],
        compiler_params=_cparams(1),
        name="nsa_compress",
    )(kc, vc, pek, pev, wk1, wk2, wv1, wv2, m)


def _rank_rows(vals, count):
    idx = lax.broadcasted_iota(jnp.int32, vals.shape, 0)
    rank = jnp.zeros(vals.shape, jnp.int32)
    for j in range(count):
        rowj = vals[j:j + 1, :]
        beats = (rowj > vals) | ((rowj == vals) & (j < idx))
        rank = rank + beats.astype(jnp.int32)
    return rank


def _tile_masks():
    kpos = lax.broadcasted_iota(jnp.int32, (TILE, TILE), 0)
    qpos = lax.broadcasted_iota(jnp.int32, (TILE, TILE), 1)
    return kpos <= qpos, kpos > qpos


def _tile_slice(j):
    return pl.ds(pl.multiple_of(j * TILE, TILE), TILE)


HALF = TILE // 2
V_ROWS = 80
QK_LEAD = 7
ACC_LAG = 2


def _attend(items, m_ref, acc_ref, fresh):
    n = len(items)
    m_val, acc_val = {}, {}
    scores, updates = {}, {}
    for t in range(n + QK_LEAD + ACC_LAG):
        if t < n:
            _, qt, k, _, mask = items[t]
            s = _dot(k, qt)
            scores[t] = s if mask is None else jnp.where(mask, s, NEG)
        i = t - QK_LEAD
        if 0 <= i < n:
            slot, _, _, vt, _ = items[i]
            s = scores.pop(i)
            if slot not in m_val and not fresh:
                m_val[slot] = m_ref[slot]
            m = m_val.get(slot)
            c0 = jnp.max(s[:HALF], axis=0, keepdims=True)
            m0 = c0 if m is None else jnp.maximum(m, c0)
            p0 = jnp.exp2(s[:HALF] - m0).astype(BF16)
            m1 = jnp.maximum(m0, jnp.max(s[HALF:], axis=0, keepdims=True))
            p1 = jnp.exp2(s[HALF:] - m1).astype(BF16)
            p0 = p0 * jnp.exp2(m0 - m1).astype(BF16)
            updates[i] = (None if m is None else jnp.exp2(m - m1),
                          _dot(vt, jnp.concatenate([p0, p1], axis=0)))
            m_val[slot] = m1
        i = t - QK_LEAD - ACC_LAG
        if 0 <= i < n:
            slot = items[i][0]
            alpha, pv = updates.pop(i)
            if alpha is None:
                acc_val[slot] = pv
            else:
                if slot not in acc_val:
                    acc_val[slot] = acc_ref[slot]
                acc_val[slot] = alpha * acc_val[slot] + pv
    for slot in m_val:
        m_ref[slot] = m_val[slot]
        acc_ref[slot] = acc_val[slot]


def _tile_items(qts, k_of, vt_of, j, mask=None, slot0=0):
    return [(slot0 + h, qt, k_of(h, j), vt_of(h, j), mask) for h, qt in enumerate(qts)]


PAST_GROUP = 4


def _causal_sweep(m_ref, acc_ref, other_items, qts, k_of, vt_of, qi, causal, slot0=0):
    def tiles(js):
        return sum((_tile_items(qts, k_of, vt_of, j, slot0=slot0) for j in js), [])

    def head_block(r):
        def run():
            _attend(other_items() + _tile_items(qts, k_of, vt_of, qi, causal, slot0)
                    + tiles([qi - r + i for i in range(r)]), m_ref, acc_ref, True)
            return jnp.int32(0)
        return run

    def trip(i, carry):
        _attend(tiles([PAST_GROUP * i + c for c in range(PAST_GROUP)]), m_ref, acc_ref, False)
        return carry

    lax.switch(qi % PAST_GROUP, [head_block(r) for r in range(PAST_GROUP)])
    lax.fori_loop(0, qi // PAST_GROUP, trip, jnp.int32(0))


def _normalize(acc):
    return acc[:HEAD_DIM] * (1.0 / acc[HEAD_DIM:HEAD_DIM + 1])


def _fill_transposed(vt_ref, v_ref, n_tiles):
    for c in range(n_tiles):
        vt_ref[c] = v_ref[c * TILE:(c + 1) * TILE, :].T


def _moba_queries(q_ref, kmean_ref, qi):
    heads = range(q_ref.shape[1])
    blk = lax.broadcasted_iota(jnp.int32, (16, TILE), 0)
    qts = [q_ref[0, h].T for h in heads]
    gscs = [jnp.where(blk < qi, _dot(kmean_ref[0, h], qts[h]), NEG) for h in heads]
    qsel = []
    for h in heads:
        keep = ((blk < qi) & (_rank_rows(gscs[h], 8) < jnp.minimum(MOBA_TOPK, qi))) | (blk == qi)
        ext = qts[h][MOBA_SEL0:MOBA_SEL0 + 16].astype(F32)
        ext = jnp.where((blk < 8) & jnp.logical_not(keep), NEG, ext).astype(BF16)
        qsel.append(jnp.concatenate([qts[h][:MOBA_SEL0], ext, qts[h][MOBA_SEL0 + 16:]], axis=0))
    return qsel


def _attn_kernel(qm_ref, km_ref, vm_ref, kmean_ref, gm_ref,
                 q_ref, ks_ref, vs_ref, kw_ref, vw_ref, kc_ref, vct_ref, bg_ref, gate_ref,
                 om_ref, o_ref, vmt_ref, vst_ref, vwt_ref, m_ref, acc_ref):
    qi = pl.program_id(1)
    n_tiles = ks_ref.shape[2] // TILE
    n_groups = ks_ref.shape[1]
    groups = range(n_groups)
    heads = range(n_groups * NSA_HPG)
    grp = lambda h: h // NSA_HPG
    n_moba = qm_ref.shape[1]

    @pl.when(qi == 0)
    def _():
        for h in range(n_moba):
            _fill_transposed(vmt_ref.at[h], vm_ref.at[0, h], n_tiles)
        for g in groups:
            _fill_transposed(vst_ref.at[g], vs_ref.at[0, g], n_tiles)
            _fill_transposed(vwt_ref.at[g], vw_ref.at[0, g], n_tiles)

    causal, anti = _tile_masks()
    qsel_m = _moba_queries(qm_ref, kmean_ref, qi)
    qts = [q_ref[0, h].T for h in heads]

    n_cmp = kc_ref.shape[2]
    t_c = qi * TILE + lax.broadcasted_iota(jnp.int32, (n_cmp, TILE), 1)
    ok_c = t_c >= lax.broadcasted_iota(jnp.int32, (n_cmp, TILE), 0) * CMP_STRIDE + (CMP_LEN - 1)
    s_cmp = [jnp.where(ok_c, _dot(kc_ref[0, grp(h)], qts[h]), NEG) for h in heads]
    r_cmp = []
    for h in heads:
        e = jnp.exp2(s_cmp[h] - jnp.max(s_cmp[h], axis=0, keepdims=True))
        p = jnp.where(ok_c, e * (1.0 / jnp.sum(e, axis=0, keepdims=True)), 0.0)
        r_cmp.append(_dot(vct_ref[0, grp(h), :NSA_POS0, :], p.astype(BF16)))

    n_sel = NSA_POS0 - NSA_SEL0
    n = lax.broadcasted_iota(jnp.int32, (n_sel, TILE), 0)
    blk = (qi * TILE + lax.broadcasted_iota(jnp.int32, (n_sel, TILE), 1)) // SLC_BLOCK
    forced = (n == 0) | (n == blk) | (n == blk - 1)
    qsel = []
    for g in groups:
        r = [r_cmp[g * NSA_HPG + j][NSA_SEL0:NSA_POS0] for j in range(NSA_HPG)]
        imp = jnp.where(forced, FORCE_BONUS, (r[0] + r[1]) + (r[2] + r[3]))
        imp = jnp.where(n <= blk, imp, NEG)
        bias = lax.cond(
            (qi + 1) * TILE > SLC_TOPN * SLC_BLOCK,
            lambda v: jnp.where((_rank_rows(v, n_sel) < SLC_TOPN) & (n <= blk), 0.0, NEG),
            lambda v: jnp.where(n <= blk, 0.0, NEG), imp).astype(BF16)
        qsel += [jnp.concatenate([qts[g * NSA_HPG + j][:NSA_SEL0], bias, qts[g * NSA_HPG + j][NSA_POS0:]],
                                 axis=0) for j in range(NSA_HPG)]

    wk = lambda h, j: kw_ref[0, grp(h), _tile_slice(jnp.maximum(j, 0)), :]
    wv = lambda h, j: vwt_ref[grp(h), jnp.maximum(j, 0), :V_ROWS, :]
    nh = len(heads)
    window_items = lambda: (_tile_items(qts, wk, wv, qi, causal) + _tile_items(qts, wk, wv, qi - 1, qi >= 1)
                            + _tile_items(qts, wk, wv, qi - 2, jnp.logical_and(anti, qi >= 2)))

    def sweep_k(h, j):
        return km_ref[0, h, _tile_slice(j), :] if h < n_moba else ks_ref[0, grp(h - n_moba), _tile_slice(j), :]

    def sweep_v(h, j):
        return vmt_ref[h, j, :V_ROWS, :] if h < n_moba else vst_ref[grp(h - n_moba), j, :V_ROWS, :]

    _causal_sweep(m_ref, acc_ref, window_items, qsel_m + qsel, sweep_k, sweep_v, qi, causal, slot0=nh)

    o = jnp.concatenate([_normalize(acc_ref[nh + h]) for h in range(n_moba)], axis=0)
    om_ref[...] = (o.T * gm_ref[...].astype(F32)).astype(BF16)

    bgs = [bg_ref[0, g].T for g in groups]
    outs = []
    for h in heads:
        bg, r0 = bgs[grp(h)], 3 * (h % NSA_HPG)
        outs.append(bg[r0:r0 + 1] * r_cmp[h][:HEAD_DIM]
                    + bg[r0 + 1:r0 + 2] * _normalize(acc_ref[nh + n_moba + h])
                    + bg[r0 + 2:r0 + 3] * _normalize(acc_ref[h]))
    o = jnp.concatenate(outs, axis=0)
    o_ref[...] = (o.T * gate_ref[...].astype(F32)).astype(BF16)


def _attention(qm, km, vm, kmean, gm, qn, ks, vs, kw, vw, kcmp, vcmp_t, gates, gn, B, S):
    nq = S // TILE
    nrow = S // CMP_STRIDE
    G = NSA_KV_GROUPS
    per_b = lambda shape: pl.BlockSpec((1,) + shape, lambda b, i: (b, 0, 0, 0))
    per_tile = lambda nh: pl.BlockSpec((1, nh, TILE, LANES), lambda b, i: (b, 0, i, 0))
    rows = lambda w: pl.BlockSpec((TILE, w), lambda b, i: (b * nq + i, 0))
    wide_m, wide_n = MOBA_HEADS * HEAD_DIM, NSA_HEADS * HEAD_DIM
    n_slots = 2 * NSA_HEADS + MOBA_HEADS
    return pl.pallas_call(
        _attn_kernel,
        grid=(B, nq),
        in_specs=[
            per_tile(MOBA_HEADS), per_b((MOBA_HEADS, S, LANES)), per_b((MOBA_HEADS, S, LANES)),
            per_b((MOBA_HEADS, 16, LANES)), rows(wide_m),
            per_tile(NSA_HEADS), per_b((G, S, LANES)), per_b((G, S, LANES)), per_b((G, S, LANES)),
            per_b((G, S, LANES)), per_b((G, nrow, LANES)), per_b((G, nrow, LANES)),
            per_tile(G), rows(wide_n),
        ],
        out_specs=(rows(wide_m), rows(wide_n)),
        out_shape=(jax.ShapeDtypeStruct((B * S, wide_m), BF16), jax.ShapeDtypeStruct((B * S, wide_n), BF16)),
        scratch_shapes=[pltpu.VMEM((MOBA_HEADS, nq, LANES, TILE), BF16),
                        pltpu.VMEM((G, nq, LANES, TILE), BF16), pltpu.VMEM((G, nq, LANES, TILE), BF16),
                        pltpu.VMEM((n_slots, 1, TILE), F32), pltpu.VMEM((n_slots, V_ROWS, TILE), F32)],
        compiler_params=_cparams(2),
        name="attention",
    )(qm, km, vm, kmean, gm, qn, ks, vs, kw, vw, kcmp, vcmp_t, gates, gn)


def _outproj_kernel(ym_ref, yn_ref, w_ref, x_ref, mod_ref, g_ref, o_ref):
    D = x_ref.shape[1]
    half = ym_ref.shape[1]
    z = _dot(ym_ref[...], w_ref[:half, :]) + _dot(yn_ref[...], w_ref[half:, :])
    r = z * lax.rsqrt(jnp.mean(z * z, axis=-1, keepdims=True) + RMS_EPS) * g_ref[...]
    o_ref[...] = x_ref[...] + mod_ref[0][:, 2 * D:] * r


def _outproj(ym, yn, w_out, x2, mod, g_post, S):
    M, D = x2.shape
    tm = ROW_TILE
    nt = S // tm
    row_spec = lambda w: pl.BlockSpec((tm, w), lambda i: (i, 0))
    return pl.pallas_call(
        _outproj_kernel,
        grid=(M // tm,),
        in_specs=[
            row_spec(ym.shape[1]), row_spec(yn.shape[1]),
            pl.BlockSpec(w_out.shape, lambda i: (0, 0)),
            row_spec(D),
            pl.BlockSpec((1, 1, 3 * D), lambda i: (i // nt, 0, 0)),
            pl.BlockSpec((1, D), lambda i: (0, 0)),
        ],
        out_specs=row_spec(D),
        out_shape=jax.ShapeDtypeStruct((M, D), F32),
        compiler_params=_cparams(1),
        name="out_proj",
    )(ym, yn, w_out, x2, mod, g_post)


def kernel(x, c, w_ada, b_ada, g_pre, g_post, w_in, w_out, pe_k, pe_v, w_ck1, w_ck2, w_cv1, w_cv2):
    B, S, D = x.shape
    L = w_ada.shape[0]
    assert D == (MOBA_HEADS + NSA_HEADS) * HEAD_DIM and S % ROW_TILE == 0
    assert S // MOBA_BLOCK <= 8 and S // SLC_BLOCK <= 32 and S // CMP_STRIDE <= LANES
    nb = S // MOBA_BLOCK

    mod_all = _modulation(c, w_ada, b_ada)
    x2 = x.reshape(B * S, D)
    for l in range(L):
        mod = mod_all[l].reshape(B, 1, 3 * D)
        (qm, km, vm, kmean, gm, qn, ks, vs, kw, vw, kc, vc, gates, gn) = _inproj(
            x2, mod, g_pre[l].reshape(1, D), _pack_w_in(w_in[l]), B, S)
        kmean = kmean.reshape(B, nb, MOBA_HEADS, HEAD_DIM).transpose(0, 2, 1, 3)
        kmean = jnp.pad(kmean, ((0, 0), (0, 0), (0, 16 - nb), (0, LANES - HEAD_DIM))).astype(BF16)
        kcmp, vcmp = _compress(kc, vc, _pack_compress_weights(pe_k[l], w_ck1[l], w_ck2[l]),
                               _pack_compress_weights(pe_v[l], w_cv1[l], w_cv2[l]), B, S)
        ym, yn = _attention(qm, km, vm, kmean, gm, qn, ks, vs, kw, vw, kcmp, vcmp, gates, gn, B, S)
        x2 = _outproj(ym, yn, w_out[l].astype(BF16), x2, mod, g_post[l].reshape(1, D), S)
    return x2.reshape(B, S, D)
```

```python
import functools

import numpy as np
import jax
import jax.numpy as jnp
from jax import lax
from jax.experimental import pallas as pl
from jax.experimental.pallas import tpu as pltpu

HEAD_DIM = 64
MOBA_HEADS = 8
NSA_HEADS = 8
NSA_KV_GROUPS = 2
NSA_HPG = NSA_HEADS // NSA_KV_GROUPS
MOBA_BLOCK = 256
MOBA_TOPK = 3
CMP_LEN = 32
CMP_STRIDE = 16
SLC_BLOCK = 64
SLC_TOPN = 16
WINDOW = 512
RMS_EPS = 1e-6
NEG = -1e9
FORCE_BONUS = 1e4

LANES = 128
TILE = 256
POS_RADIX = 16
ROW_TILE = 512
VMEM_LIMIT = 56 * 1024 * 1024

MOBA_SEL0 = HEAD_DIM
MOBA_POS0 = MOBA_SEL0 + 8
NSA_SEL0 = HEAD_DIM
NSA_POS0 = NSA_SEL0 + 32
N_POS = 8
LOG2E = float(np.log2(np.e))

BF16 = jnp.bfloat16
F32 = jnp.float32


def _slope(i, n):
    return 2.0 ** (-8.0 * (i + 1) / n)


def _alibi_coef():
    table = np.zeros((MOBA_HEADS + NSA_HEADS, LANES), np.float32)
    for row in range(MOBA_HEADS + NSA_HEADS):
        moba = row < MOBA_HEADS
        c = (_slope(row, MOBA_HEADS) if moba else _slope(row - MOBA_HEADS, NSA_HEADS)) * LOG2E
        parts, rest = [], c
        for _ in range(3):
            part = float(np.float32(rest).astype(BF16).astype(np.float32))
            parts.append(part)
            rest -= part
        p0 = MOBA_POS0 if moba else NSA_POS0
        table[row, p0:p0 + N_POS] = [parts[0], parts[0], parts[1], parts[1], parts[2], parts[2], c, c]
    return jnp.asarray(table)


def _dot(a, b):
    return jnp.dot(a, b, preferred_element_type=F32)


def _sigmoid(x):
    return 1.0 / (1.0 + jnp.exp(-x))


def _cparams(n_axes):
    return pltpu.CompilerParams(
        dimension_semantics=("arbitrary",) * n_axes, vmem_limit_bytes=VMEM_LIMIT)


def _mod_kernel(c_ref, w_ref, b_ref, o_ref):
    c = c_ref[...]
    cs = (c * _sigmoid(c)).astype(BF16)
    o_ref[0] = _dot(cs, w_ref[0].astype(BF16)) + b_ref[0]


def _modulation(c, w_ada, b_ada):
    L, D, N = w_ada.shape
    B = c.shape[0]
    tn = 512
    return pl.pallas_call(
        _mod_kernel,
        grid=(L, N // tn),
        in_specs=[
            pl.BlockSpec((B, D), lambda l, j: (0, 0)),
            pl.BlockSpec((1, D, tn), lambda l, j: (l, 0, j)),
            pl.BlockSpec((1, 1, tn), lambda l, j: (l, 0, j)),
        ],
        out_specs=pl.BlockSpec((1, B, tn), lambda l, j: (l, 0, j)),
        out_shape=jax.ShapeDtypeStruct((L, B, N), F32),
        compiler_params=_cparams(2),
        name="adaln_mod",
    )(c, w_ada, b_ada.reshape(L, 1, N))


_SEG = {}
_off = 0
for _name, _w in (("qm", 512), ("km", 512), ("vm", 512), ("zm", 512), ("qn", 512),
                  ("kc", 128), ("vc", 128), ("ks", 128), ("vs", 128), ("kw", 128),
                  ("vw", 128), ("zn", 512), ("gl", 128)):
    _SEG[_name] = (_off, _w)
    _off += _w
D_IN_PACKED = _off


def _pack_w_in(w_in):
    offs = np.cumsum([0, 512, 512, 512, 512, 512, 128, 128, 128, 128, 128, 128, 24, 512])
    parts = [w_in[:, offs[i]:offs[i + 1]] for i in range(13)]
    gl, zn = parts[11], parts[12]
    gl = jnp.pad(gl, ((0, 0), (0, LANES - gl.shape[1])))
    return jnp.concatenate(parts[:11] + [zn, gl], axis=1).astype(BF16)


def _inproj_kernel(x_ref, mod_ref, g_ref, w_ref, coef_ref,
                   qm_ref, km_ref, vm_ref, kmean_ref, gm_ref,
                   qn_ref, ks_ref, vs_ref, kw_ref, vw_ref,
                   kc_ref, vc_ref, gate_ref, gn_ref, *, seq_len):
    tm, D = x_ref.shape
    x = x_ref[...]
    mod = mod_ref[0]
    shift, scale = mod[:, :D], mod[:, D:2 * D]
    y = x * lax.rsqrt(jnp.mean(x * x, axis=-1, keepdims=True) + RMS_EPS) * g_ref[...]
    hb = (y * (1.0 + scale) + shift).astype(BF16)

    def seg(name):
        o, w = _SEG[name]
        return _dot(hb, w_ref[:, o:o + w])

    lane = lax.broadcasted_iota(jnp.int32, (tm, LANES), 1)
    row = lax.broadcasted_iota(jnp.int32, (tm, LANES), 0)
    t = (pl.program_id(0) % (seq_len // tm)) * tm + row
    t_hi = (t // POS_RADIX).astype(F32)
    t_lo = (t % POS_RADIX).astype(F32)

    def pos_lanes(p0, vals):
        out = jnp.zeros((tm, LANES), F32)
        for i, v in enumerate(vals):
            out = jnp.where(lane == p0 + i, v, out)
        return out

    q_pat = [float(POS_RADIX), 1.0] * 3 + [-POS_RADIX * t_hi, -t_lo]
    k_pat = [t_hi, t_lo] * 3 + [1.0, 1.0]
    q_pos_m, k_pos_m = pos_lanes(MOBA_POS0, q_pat), pos_lanes(MOBA_POS0, k_pat)
    q_pos_n, k_pos_n = pos_lanes(NSA_POS0, q_pat), pos_lanes(NSA_POS0, k_pat)
    k_ext_m = k_pos_m + jnp.where(lane - MOBA_SEL0 == t // MOBA_BLOCK, 1.0, 0.0)
    k_ext_s = k_pos_n + jnp.where(lane - NSA_SEL0 == t // SLC_BLOCK, 1.0, 0.0)
    v_ext = jnp.where(lane == HEAD_DIM, 1.0, 0.0)
    low = lane < HEAD_DIM

    def head(u, h, ext):
        p = u[:, (h // 2) * LANES:(h // 2 + 1) * LANES]
        if h % 2:
            p = pltpu.roll(p, HEAD_DIM, axis=1)
        return jnp.where(low, p, ext).astype(BF16)

    u = seg("qm") * (HEAD_DIM ** -0.5 * LOG2E)
    for h in range(MOBA_HEADS):
        qm_ref[0, h] = head(u, h, coef_ref[h:h + 1, :] * q_pos_m)
    u = seg("km")
    for h in range(MOBA_HEADS):
        km_ref[0, h] = head(u, h, k_ext_m)
    for r in range(tm // MOBA_BLOCK):
        kmean_ref[0, r] = jnp.mean(u[r * MOBA_BLOCK:(r + 1) * MOBA_BLOCK], axis=0, keepdims=True)
    u = seg("vm")
    for h in range(MOBA_HEADS):
        vm_ref[0, h] = head(u, h, v_ext)
    u = seg("zm")
    gm_ref[...] = (u * _sigmoid(u)).astype(BF16)
    u = seg("qn") * (HEAD_DIM ** -0.5 * LOG2E)
    for h in range(NSA_HEADS):
        qn_ref[0, h] = head(u, h, coef_ref[MOBA_HEADS + h:MOBA_HEADS + h + 1, :] * q_pos_n)
    for name, ref, ext in (("ks", ks_ref, k_ext_s), ("vs", vs_ref, v_ext),
                           ("kw", kw_ref, k_pos_n), ("vw", vw_ref, v_ext)):
        u = seg(name)
        for g in range(NSA_KV_GROUPS):
            ref[0, g] = head(u, g, ext)
    kc_ref[...] = seg("kc")
    vc_ref[...] = seg("vc")
    u = seg("zn")
    gn_ref[...] = (u * _sigmoid(u)).astype(BF16)
    sg = _sigmoid(seg("gl"))
    for g in range(NSA_KV_GROUPS):
        gate_ref[0, g] = sg if g == 0 else pltpu.roll(sg, LANES - g * 3 * NSA_HPG, axis=1)


def _inproj(x2, mod, g_pre, w_packed, B, S):
    M, D = x2.shape
    tm = ROW_TILE
    nt = S // tm
    grid = (M // tm,)
    bs = lambda i: i // nt
    si = lambda i: i % nt
    head_spec = lambda nh: pl.BlockSpec((1, nh, tm, LANES), lambda i: (bs(i), 0, si(i), 0))
    row_spec = lambda w: pl.BlockSpec((tm, w), lambda i: (i, 0))
    sds = jax.ShapeDtypeStruct
    out_shape = (
        sds((B, MOBA_HEADS, S, LANES), BF16),
        sds((B, MOBA_HEADS, S, LANES), BF16),
        sds((B, MOBA_HEADS, S, LANES), BF16),
        sds((B, S // MOBA_BLOCK, 1, 512), F32),
        sds((M, 512), BF16),
        sds((B, NSA_HEADS, S, LANES), BF16),
        sds((B, NSA_KV_GROUPS, S, LANES), BF16),
        sds((B, NSA_KV_GROUPS, S, LANES), BF16),
        sds((B, NSA_KV_GROUPS, S, LANES), BF16),
        sds((B, NSA_KV_GROUPS, S, LANES), BF16),
        sds((M, LANES), F32),
        sds((M, LANES), F32),
        sds((B, NSA_KV_GROUPS, S, LANES), F32),
        sds((M, 512), BF16),
    )
    out_specs = (
        head_spec(MOBA_HEADS), head_spec(MOBA_HEADS), head_spec(MOBA_HEADS),
        pl.BlockSpec((1, tm // MOBA_BLOCK, 1, 512), lambda i: (bs(i), si(i), 0, 0)),
        row_spec(512),
        head_spec(NSA_HEADS),
        head_spec(NSA_KV_GROUPS), head_spec(NSA_KV_GROUPS),
        head_spec(NSA_KV_GROUPS), head_spec(NSA_KV_GROUPS),
        row_spec(LANES), row_spec(LANES),
        head_spec(NSA_KV_GROUPS),
        row_spec(512),
    )
    return pl.pallas_call(
        functools.partial(_inproj_kernel, seq_len=S),
        grid=grid,
        in_specs=[
            row_spec(D),
            pl.BlockSpec((1, 1, 3 * D), lambda i: (bs(i), 0, 0)),
            pl.BlockSpec((1, D), lambda i: (0, 0)),
            pl.BlockSpec((D, D_IN_PACKED), lambda i: (0, 0)),
            pl.BlockSpec((MOBA_HEADS + NSA_HEADS, LANES), lambda i: (0, 0)),
        ],
        out_specs=out_specs,
        out_shape=out_shape,
        compiler_params=_cparams(1),
        name="in_proj",
    )(x2, mod, g_pre, w_packed, _alibi_coef())


def _gelu_tanh(x):
    return 0.5 * x * (1.0 + jnp.tanh(np.sqrt(2.0 / np.pi) * (x + 0.044715 * (x * x * x))))


def _compress_kernel(kc_ref, vc_ref, pek_ref, pev_ref, wk1_ref, wk2_ref, wv1_ref, wv2_ref,
                     m_ref, kout_ref, vout_ref, win_ref):
    S = kc_ref.shape[0]
    nrow = S // CMP_STRIDE
    lane = lax.broadcasted_iota(jnp.int32, (nrow, LANES), 1)
    row = lax.broadcasted_iota(jnp.int32, (nrow, LANES), 0)
    low = lane < HEAD_DIM
    valid = row < nrow - 1
    pos = lane - NSA_POS0
    k_ext = jnp.where((pos >= 0) & (pos < N_POS - 2),
                      jnp.where(pos % 2 == 0, (row + 1).astype(F32), float(POS_RADIX - 1)),
                      jnp.where((pos == N_POS - 2) | (pos == N_POS - 1), 1.0, 0.0))

    def run(src_ref, pe_ref, w1_ref, w2_ref, out_ref, ext, transposed):
        for l in range(CMP_STRIDE):
            win_ref[:, l * LANES:(l + 1) * LANES] = src_ref[pl.ds(l, nrow, stride=CMP_STRIDE), :]
        xw = win_ref[...]
        lo = _dot((xw + pe_ref[0:1, :]).astype(BF16), w1_ref[0])
        hi = _dot((xw + pe_ref[1:2, :]).astype(BF16), w1_ref[1])
        pre = lo + pltpu.roll(hi, nrow - 1, axis=0)
        out = _dot(_gelu_tanh(pre).astype(BF16), w2_ref[...])
        out = jnp.where(valid, out, 0.0)
        for g in range(NSA_KV_GROUPS):
            p = out if g == 0 else pltpu.roll(out, HEAD_DIM, axis=1)
            p = jnp.where(low, p, ext)
            out_ref[0, g] = (p.T if transposed else p).astype(BF16)

    run(kc_ref, pek_ref, wk1_ref, wk2_ref, kout_ref, k_ext, False)
    run(vc_ref, pev_ref, wv1_ref, wv2_ref, vout_ref, jnp.where(valid, m_ref[...], 0.0), True)


def _pack_compress_weights(pe, w1, w2):
    half = CMP_STRIDE
    eye = jnp.eye(NSA_KV_GROUPS, dtype=F32)
    w1r = w1.reshape(2, half, HEAD_DIM, HEAD_DIM)
    w1bd = jnp.einsum("hlde,gk->hlgdke", w1r, eye).reshape(2, half * LANES, LANES).astype(BF16)
    w2bd = jnp.einsum("de,gk->gdke", w2, eye).reshape(LANES, LANES).astype(BF16)
    pet = jnp.tile(pe.reshape(2, half, 1, HEAD_DIM), (1, 1, NSA_KV_GROUPS, 1)).reshape(2, half * LANES)
    return pet, w1bd, w2bd


def _cmp_to_slc_lanes(n_rows, n_slc):
    i = np.arange(n_rows)[:, None]
    j = np.arange(n_slc)[None, :]
    start = i * CMP_STRIDE
    ov = (start < (j + 1) * SLC_BLOCK) & (start + CMP_LEN > j * SLC_BLOCK)
    m = np.zeros((n_rows, LANES), np.float32)
    m[:, NSA_SEL0:NSA_SEL0 + n_slc] = ov
    return jnp.asarray(m)


def _compress(kc, vc, pk, pv, B, S):
    nrow = S // CMP_STRIDE
    pek, wk1, wk2 = pk
    pev, wv1, wv2 = pv
    m = _cmp_to_slc_lanes(nrow, S // SLC_BLOCK)
    full = lambda a: pl.BlockSpec(a.shape, lambda b: (0,) * a.ndim)
    out_spec = pl.BlockSpec((1, NSA_KV_GROUPS, nrow, LANES), lambda b: (b, 0, 0, 0))
    out_sds = jax.ShapeDtypeStruct((B, NSA_KV_GROUPS, nrow, LANES), BF16)
    return pl.pallas_call(
        _compress_kernel,
        grid=(B,),
        in_specs=[pl.BlockSpec((S, LANES), lambda b: (b, 0)), pl.BlockSpec((S, LANES), lambda b: (b, 0)),
                  full(pek), full(pev), full(wk1), full(wk2), full(wv1), full(wv2), full(m)],
        out_specs=(out_spec, out_spec),
        out_shape=(out_sds, out_sds),
        scratch_shapes=[pltpu.VMEM((nrow, CMP_STRIDE * LANES), F32)],
        compiler_params=_cparams(1),
        name="nsa_compress",
    )(kc, vc, pek, pev, wk1, wk2, wv1, wv2, m)


def _rank_rows(vals, count):
    idx = lax.broadcasted_iota(jnp.int32, vals.shape, 0)
    rank = jnp.zeros(vals.shape, jnp.int32)
    for j in range(count):
        rowj = vals[j:j + 1, :]
        beats = (rowj > vals) | ((rowj == vals) & (j < idx))
        rank = rank + beats.astype(jnp.int32)
    return rank


def _tile_masks():
    kpos = lax.broadcasted_iota(jnp.int32, (TILE, TILE), 0)
    qpos = lax.broadcasted_iota(jnp.int32, (TILE, TILE), 1)
    return kpos <= qpos, kpos > qpos


def _tile_slice(j):
    return pl.ds(pl.multiple_of(j * TILE, TILE), TILE)


HALF = TILE // 2
V_ROWS = 80
QK_LEAD = 7
ACC_LAG = 2


def _attend(items, m_ref, acc_ref, fresh):
    n = len(items)
    m_val, acc_val = {}, {}
    scores, updates = {}, {}
    for t in range(n + QK_LEAD + ACC_LAG):
        if t < n:
            _, qt, k, _, mask = items[t]
            s = _dot(k, qt)
            scores[t] = s if mask is None else jnp.where(mask, s, NEG)
        i = t - QK_LEAD
        if 0 <= i < n:
            slot, _, _, vt, _ = items[i]
            s = scores.pop(i)
            if slot not in m_val and not fresh:
                m_val[slot] = m_ref[slot]
            m = m_val.get(slot)
            c0 = jnp.max(s[:HALF], axis=0, keepdims=True)
            m0 = c0 if m is None else jnp.maximum(m, c0)
            p0 = jnp.exp2(s[:HALF] - m0).astype(BF16)
            m1 = jnp.maximum(m0, jnp.max(s[HALF:], axis=0, keepdims=True))
            p1 = jnp.exp2(s[HALF:] - m1).astype(BF16)
            p0 = p0 * jnp.exp2(m0 - m1).astype(BF16)
            updates[i] = (None if m is None else jnp.exp2(m - m1),
                          _dot(vt, jnp.concatenate([p0, p1], axis=0)))
            m_val[slot] = m1
        i = t - QK_LEAD - ACC_LAG
        if 0 <= i < n:
            slot = items[i][0]
            alpha, pv = updates.pop(i)
            if alpha is None:
                acc_val[slot] = pv
            else:
                if slot not in acc_val:
                    acc_val[slot] = acc_ref[slot]
                acc_val[slot] = alpha * acc_val[slot] + pv
    for slot in m_val:
        m_ref[slot] = m_val[slot]
        acc_ref[slot] = acc_val[slot]


def _tile_items(qts, k_of, vt_of, j, mask=None, slot0=0):
    return [(slot0 + h, qt, k_of(h, j), vt_of(h, j), mask) for h, qt in enumerate(qts)]


PAST_GROUP = 4


def _causal_sweep(m_ref, acc_ref, other_items, qts, k_of, vt_of, qi, causal, slot0=0):
    def tiles(js):
        return sum((_tile_items(qts, k_of, vt_of, j, slot0=slot0) for j in js), [])

    def head_block(r):
        def run():
            _attend(other_items() + _tile_items(qts, k_of, vt_of, qi, causal, slot0)
                    + tiles([qi - r + i for i in range(r)]), m_ref, acc_ref, True)
            return jnp.int32(0)
        return run

    def trip(i, carry):
        _attend(tiles([PAST_GROUP * i + c for c in range(PAST_GROUP)]), m_ref, acc_ref, False)
        return carry

    lax.switch(qi % PAST_GROUP, [head_block(r) for r in range(PAST_GROUP)])
    lax.fori_loop(0, qi // PAST_GROUP, trip, jnp.int32(0))


def _normalize(acc):
    return acc[:HEAD_DIM] * (1.0 / acc[HEAD_DIM:HEAD_DIM + 1])


def _fill_transposed(vt_ref, v_ref, n_tiles):
    for c in range(n_tiles):
        vt_ref[c] = v_ref[c * TILE:(c + 1) * TILE, :].T


def _moba_queries(q_ref, kmean_ref, qi):
    heads = range(q_ref.shape[1])
    blk = lax.broadcasted_iota(jnp.int32, (16, TILE), 0)
    qts = [q_ref[0, h].T for h in heads]
    gscs = [jnp.where(blk < qi, _dot(kmean_ref[0, h], qts[h]), NEG) for h in heads]
    qsel = []
    for h in heads:
        keep = ((blk < qi) & (_rank_rows(gscs[h], 8) < jnp.minimum(MOBA_TOPK, qi))) | (blk == qi)
        ext = qts[h][MOBA_SEL0:MOBA_SEL0 + 16].astype(F32)
        ext = jnp.where((blk < 8) & jnp.logical_not(keep), NEG, ext).astype(BF16)
        qsel.append(jnp.concatenate([qts[h][:MOBA_SEL0], ext, qts[h][MOBA_SEL0 + 16:]], axis=0))
    return qsel


def _attn_kernel(qm_ref, km_ref, vm_ref, kmean_ref, gm_ref,
                 q_ref, ks_ref, vs_ref, kw_ref, vw_ref, kc_ref, vct_ref, bg_ref, gate_ref,
                 w_ref, x_ref, mod_ref, g_ref,
                 x_out_ref, vmt_ref, vst_ref, vwt_ref, m_ref, acc_ref):
    qi = pl.program_id(1)
    n_tiles = ks_ref.shape[2] // TILE
    n_groups = ks_ref.shape[1]
    groups = range(n_groups)
    heads = range(n_groups * NSA_HPG)
    grp = lambda h: h // NSA_HPG
    n_moba = qm_ref.shape[1]

    @pl.when(qi == 0)
    def _():
        for h in range(n_moba):
            _fill_transposed(vmt_ref.at[h], vm_ref.at[0, h], n_tiles)
        for g in groups:
            _fill_transposed(vst_ref.at[g], vs_ref.at[0, g], n_tiles)
            _fill_transposed(vwt_ref.at[g], vw_ref.at[0, g], n_tiles)

    causal, anti = _tile_masks()
    qsel_m = _moba_queries(qm_ref, kmean_ref, qi)
    qts = [q_ref[0, h].T for h in heads]

    n_cmp = kc_ref.shape[2]
    t_c = qi * TILE + lax.broadcasted_iota(jnp.int32, (n_cmp, TILE), 1)
    ok_c = t_c >= lax.broadcasted_iota(jnp.int32, (n_cmp, TILE), 0) * CMP_STRIDE + (CMP_LEN - 1)
    s_cmp = [jnp.where(ok_c, _dot(kc_ref[0, grp(h)], qts[h]), NEG) for h in heads]
    r_cmp = []
    for h in heads:
        e = jnp.exp2(s_cmp[h] - jnp.max(s_cmp[h], axis=0, keepdims=True))
        p = jnp.where(ok_c, e * (1.0 / jnp.sum(e, axis=0, keepdims=True)), 0.0)
        r_cmp.append(_dot(vct_ref[0, grp(h), :NSA_POS0, :], p.astype(BF16)))

    n_sel = NSA_POS0 - NSA_SEL0
    n = lax.broadcasted_iota(jnp.int32, (n_sel, TILE), 0)
    blk = (qi * TILE + lax.broadcasted_iota(jnp.int32, (n_sel, TILE), 1)) // SLC_BLOCK
    forced = (n == 0) | (n == blk) | (n == blk - 1)
    qsel = []
    for g in groups:
        r = [r_cmp[g * NSA_HPG + j][NSA_SEL0:NSA_POS0] for j in range(NSA_HPG)]
        imp = jnp.where(forced, FORCE_BONUS, (r[0] + r[1]) + (r[2] + r[3]))
        imp = jnp.where(n <= blk, imp, NEG)
        bias = lax.cond(
            (qi + 1) * TILE > SLC_TOPN * SLC_BLOCK,
            lambda v: jnp.where((_rank_rows(v, n_sel) < SLC_TOPN) & (n <= blk), 0.0, NEG),
            lambda v: jnp.where(n <= blk, 0.0, NEG), imp).astype(BF16)
        qsel += [jnp.concatenate([qts[g * NSA_HPG + j][:NSA_SEL0], bias, qts[g * NSA_HPG + j][NSA_POS0:]],
                                 axis=0) for j in range(NSA_HPG)]

    wk = lambda h, j: kw_ref[0, grp(h), _tile_slice(jnp.maximum(j, 0)), :]
    wv = lambda h, j: vwt_ref[grp(h), jnp.maximum(j, 0), :V_ROWS, :]
    nh = len(heads)
    window_items = lambda: (_tile_items(qts, wk, wv, qi, causal) + _tile_items(qts, wk, wv, qi - 1, qi >= 1)
                            + _tile_items(qts, wk, wv, qi - 2, jnp.logical_and(anti, qi >= 2)))

    def sweep_k(h, j):
        return km_ref[0, h, _tile_slice(j), :] if h < n_moba else ks_ref[0, grp(h - n_moba), _tile_slice(j), :]

    def sweep_v(h, j):
        return vmt_ref[h, j, :V_ROWS, :] if h < n_moba else vst_ref[grp(h - n_moba), j, :V_ROWS, :]

    _causal_sweep(m_ref, acc_ref, window_items, qsel_m + qsel, sweep_k, sweep_v, qi, causal, slot0=nh)

    o = jnp.concatenate([_normalize(acc_ref[nh + h]) for h in range(n_moba)], axis=0)
    y_m = (o.T * gm_ref[...].astype(F32)).astype(BF16)

    bgs = [bg_ref[0, g].T for g in groups]
    outs = []
    for h in heads:
        bg, r0 = bgs[grp(h)], 3 * (h % NSA_HPG)
        outs.append(bg[r0:r0 + 1] * r_cmp[h][:HEAD_DIM]
                    + bg[r0 + 1:r0 + 2] * _normalize(acc_ref[nh + n_moba + h])
                    + bg[r0 + 2:r0 + 3] * _normalize(acc_ref[h]))
    o = jnp.concatenate(outs, axis=0)
    y_n = (o.T * gate_ref[...].astype(F32)).astype(BF16)

    D = x_ref.shape[1]
    z = _dot(y_m, w_ref[:y_m.shape[1], :]) + _dot(y_n, w_ref[y_m.shape[1]:, :])
    r = z * lax.rsqrt(jnp.mean(z * z, axis=-1, keepdims=True) + RMS_EPS) * g_ref[...]
    x_out_ref[...] = x_ref[...] + mod_ref[0][:, 2 * D:] * r


def _attention(qm, km, vm, kmean, gm, qn, ks, vs, kw, vw, kcmp, vcmp_t, gates, gn,
               w_out, x2, mod, g_post, B, S):
    nq = S // TILE
    D = x2.shape[1]
    nrow = S // CMP_STRIDE
    G = NSA_KV_GROUPS
    per_b = lambda shape: pl.BlockSpec((1,) + shape, lambda b, i: (b, 0, 0, 0))
    per_tile = lambda nh: pl.BlockSpec((1, nh, TILE, LANES), lambda b, i: (b, 0, i, 0))
    rows = lambda w: pl.BlockSpec((TILE, w), lambda b, i: (b * nq + i, 0))
    wide_m, wide_n = MOBA_HEADS * HEAD_DIM, NSA_HEADS * HEAD_DIM
    n_slots = 2 * NSA_HEADS + MOBA_HEADS
    return pl.pallas_call(
        _attn_kernel,
        grid=(B, nq),
        in_specs=[
            per_tile(MOBA_HEADS), per_b((MOBA_HEADS, S, LANES)), per_b((MOBA_HEADS, S, LANES)),
            per_b((MOBA_HEADS, 16, LANES)), rows(wide_m),
            per_tile(NSA_HEADS), per_b((G, S, LANES)), per_b((G, S, LANES)), per_b((G, S, LANES)),
            per_b((G, S, LANES)), per_b((G, nrow, LANES)), per_b((G, nrow, LANES)),
            per_tile(G), rows(wide_n),
            pl.BlockSpec(w_out.shape, lambda b, i: (0, 0)), rows(D),
            pl.BlockSpec((1, 1, 3 * D), lambda b, i: (b, 0, 0)), pl.BlockSpec((1, D), lambda b, i: (0, 0)),
        ],
        out_specs=rows(D),
        out_shape=jax.ShapeDtypeStruct((B * S, D), F32),
        scratch_shapes=[pltpu.VMEM((MOBA_HEADS, nq, LANES, TILE), BF16),
                        pltpu.VMEM((G, nq, LANES, TILE), BF16), pltpu.VMEM((G, nq, LANES, TILE), BF16),
                        pltpu.VMEM((n_slots, 1, TILE), F32), pltpu.VMEM((n_slots, V_ROWS, TILE), F32)],
        compiler_params=_cparams(2),
        name="attention",
    )(qm, km, vm, kmean, gm, qn, ks, vs, kw, vw, kcmp, vcmp_t, gates, gn, w_out, x2, mod, g_post)


def kernel(x, c, w_ada, b_ada, g_pre, g_post, w_in, w_out, pe_k, pe_v, w_ck1, w_ck2, w_cv1, w_cv2):
    B, S, D = x.shape
    L = w_ada.shape[0]
    assert D == (MOBA_HEADS + NSA_HEADS) * HEAD_DIM and S % ROW_TILE == 0
    assert S // MOBA_BLOCK <= 8 and S // SLC_BLOCK <= 32 and S // CMP_STRIDE <= LANES
    nb = S // MOBA_BLOCK

    mod_all = _modulation(c, w_ada, b_ada)
    x2 = x.reshape(B * S, D)
    for l in range(L):
        mod = mod_all[l].reshape(B, 1, 3 * D)
        (qm, km, vm, kmean, gm, qn, ks, vs, kw, vw, kc, vc, gates, gn) = _inproj(
            x2, mod, g_pre[l].reshape(1, D), _pack_w_in(w_in[l]), B, S)
        kmean = kmean.reshape(B, nb, MOBA_HEADS, HEAD_DIM).transpose(0, 2, 1, 3)
        kmean = jnp.pad(kmean, ((0, 0), (0, 0), (0, 16 - nb), (0, LANES - HEAD_DIM))).astype(BF16)
        kcmp, vcmp = _compress(kc, vc, _pack_compress_weights(pe_k[l], w_ck1[l], w_ck2[l]),
                               _pack_compress_weights(pe_v[l], w_cv1[l], w_cv2[l]), B, S)
        x2 = _attention(qm, km, vm, kmean, gm, qn, ks, vs, kw, vw, kcmp, vcmp, gates, gn,
                        w_out[l].astype(BF16), x2, mod, g_post[l].reshape(1, D), B, S)
    return x2.reshape(B, S, D)
```

```python
import functools

import numpy as np
import jax
import jax.numpy as jnp
from jax import lax
from jax.experimental import pallas as pl
from jax.experimental.pallas import tpu as pltpu

HEAD_DIM = 64
MOBA_HEADS = 8
NSA_HEADS = 8
NSA_KV_GROUPS = 2
NSA_HPG = NSA_HEADS // NSA_KV_GROUPS
MOBA_BLOCK = 256
MOBA_TOPK = 3
CMP_LEN = 32
CMP_STRIDE = 16
SLC_BLOCK = 64
SLC_TOPN = 16
WINDOW = 512
RMS_EPS = 1e-6
NEG = -1e9
FORCE_BONUS = 1e4

LANES = 128
TILE = 256
POS_RADIX = 16
ROW_TILE = 512
VMEM_LIMIT = 56 * 1024 * 1024

MOBA_SEL0 = HEAD_DIM
MOBA_POS0 = MOBA_SEL0 + 8
NSA_SEL0 = HEAD_DIM
NSA_POS0 = NSA_SEL0 + 32
N_POS = 8
LOG2E = float(np.log2(np.e))

BF16 = jnp.bfloat16
F32 = jnp.float32


def _slope(i, n):
    return 2.0 ** (-8.0 * (i + 1) / n)


def _alibi_coef():
    table = np.zeros((MOBA_HEADS + NSA_HEADS, LANES), np.float32)
    for row in range(MOBA_HEADS + NSA_HEADS):
        moba = row < MOBA_HEADS
        c = (_slope(row, MOBA_HEADS) if moba else _slope(row - MOBA_HEADS, NSA_HEADS)) * LOG2E
        parts, rest = [], c
        for _ in range(3):
            part = float(np.float32(rest).astype(BF16).astype(np.float32))
            parts.append(part)
            rest -= part
        p0 = MOBA_POS0 if moba else NSA_POS0
        table[row, p0:p0 + N_POS] = [parts[0], parts[0], parts[1], parts[1], parts[2], parts[2], c, c]
    return jnp.asarray(table)


def _dot(a, b):
    return jnp.dot(a, b, preferred_element_type=F32)


def _sigmoid(x):
    return 1.0 / (1.0 + jnp.exp(-x))


def _cparams(n_axes):
    return pltpu.CompilerParams(
        dimension_semantics=("arbitrary",) * n_axes, vmem_limit_bytes=VMEM_LIMIT)


def _mod_kernel(c_ref, w_ref, b_ref, o_ref):
    c = c_ref[...]
    cs = (c * _sigmoid(c)).astype(BF16)
    o_ref[0] = _dot(cs, w_ref[0].astype(BF16)) + b_ref[0]


def _modulation(c, w_ada, b_ada):
    L, D, N = w_ada.shape
    B = c.shape[0]
    tn = 512
    return pl.pallas_call(
        _mod_kernel,
        grid=(L, N // tn),
        in_specs=[
            pl.BlockSpec((B, D), lambda l, j: (0, 0)),
            pl.BlockSpec((1, D, tn), lambda l, j: (l, 0, j)),
            pl.BlockSpec((1, 1, tn), lambda l, j: (l, 0, j)),
        ],
        out_specs=pl.BlockSpec((1, B, tn), lambda l, j: (l, 0, j)),
        out_shape=jax.ShapeDtypeStruct((L, B, N), F32),
        compiler_params=_cparams(2),
        name="adaln_mod",
    )(c, w_ada, b_ada.reshape(L, 1, N))


_SEG = {}
_off = 0
for _name, _w in (("qm", 512), ("km", 512), ("vm", 512), ("zm", 512), ("qn", 512),
                  ("kc", 128), ("vc", 128), ("ks", 128), ("vs", 128), ("kw", 128),
                  ("vw", 128), ("zn", 512), ("gl", 128)):
    _SEG[_name] = (_off, _w)
    _off += _w
D_IN_PACKED = _off


def _pack_w_in(w_in):
    offs = np.cumsum([0, 512, 512, 512, 512, 512, 128, 128, 128, 128, 128, 128, 24, 512])
    parts = [w_in[:, offs[i]:offs[i + 1]] for i in range(13)]
    gl, zn = parts[11], parts[12]
    gl = jnp.pad(gl, ((0, 0), (0, LANES - gl.shape[1])))
    return jnp.concatenate(parts[:11] + [zn, gl], axis=1).astype(BF16)


def _inproj_kernel(x_ref, mod_ref, g_ref, w_ref, coef_ref,
                   qm_ref, km_ref, vm_ref, kmean_ref, gm_ref,
                   qn_ref, ks_ref, vs_ref, kw_ref, vw_ref,
                   kc_ref, vc_ref, gate_ref, gn_ref, *, seq_len):
    tm, D = x_ref.shape
    x = x_ref[...]
    mod = mod_ref[0]
    shift, scale = mod[:, :D], mod[:, D:2 * D]
    y = x * lax.rsqrt(jnp.mean(x * x, axis=-1, keepdims=True) + RMS_EPS) * g_ref[...]
    hb = (y * (1.0 + scale) + shift).astype(BF16)

    def seg(name):
        o, w = _SEG[name]
        return _dot(hb, w_ref[:, o:o + w])

    lane = lax.broadcasted_iota(jnp.int32, (tm, LANES), 1)
    row = lax.broadcasted_iota(jnp.int32, (tm, LANES), 0)
    t = (pl.program_id(0) % (seq_len // tm)) * tm + row
    t_hi = (t // POS_RADIX).astype(F32)
    t_lo = (t % POS_RADIX).astype(F32)

    def pos_lanes(p0, vals):
        out = jnp.zeros((tm, LANES), F32)
        for i, v in enumerate(vals):
            out = jnp.where(lane == p0 + i, v, out)
        return out

    q_pat = [float(POS_RADIX), 1.0] * 3 + [-POS_RADIX * t_hi, -t_lo]
    k_pat = [t_hi, t_lo] * 3 + [1.0, 1.0]
    q_pos_m, k_pos_m = pos_lanes(MOBA_POS0, q_pat), pos_lanes(MOBA_POS0, k_pat)
    q_pos_n, k_pos_n = pos_lanes(NSA_POS0, q_pat), pos_lanes(NSA_POS0, k_pat)
    k_ext_m = k_pos_m + jnp.where(lane - MOBA_SEL0 == t // MOBA_BLOCK, 1.0, 0.0)
    k_ext_s = k_pos_n + jnp.where(lane - NSA_SEL0 == t // SLC_BLOCK, 1.0, 0.0)
    v_ext = jnp.where(lane == HEAD_DIM, 1.0, 0.0)
    low = lane < HEAD_DIM

    def head(u, h, ext):
        p = u[:, (h // 2) * LANES:(h // 2 + 1) * LANES]
        if h % 2:
            p = pltpu.roll(p, HEAD_DIM, axis=1)
        return jnp.where(low, p, ext).astype(BF16)

    u = seg("qm") * (HEAD_DIM ** -0.5 * LOG2E)
    for h in range(MOBA_HEADS):
        qm_ref[0, h] = head(u, h, coef_ref[h:h + 1, :] * q_pos_m).T
    u = seg("km")
    for h in range(MOBA_HEADS):
        km_ref[0, h] = head(u, h, k_ext_m)
    for r in range(tm // MOBA_BLOCK):
        kmean_ref[0, r] = jnp.mean(u[r * MOBA_BLOCK:(r + 1) * MOBA_BLOCK], axis=0, keepdims=True)
    u = seg("vm")
    for h in range(MOBA_HEADS):
        vm_ref[0, h] = head(u, h, v_ext)
    u = seg("zm")
    gm_ref[...] = (u * _sigmoid(u)).astype(BF16)
    u = seg("qn") * (HEAD_DIM ** -0.5 * LOG2E)
    for h in range(NSA_HEADS):
        qn_ref[0, h] = head(u, h, coef_ref[MOBA_HEADS + h:MOBA_HEADS + h + 1, :] * q_pos_n).T
    for name, ref, ext in (("ks", ks_ref, k_ext_s), ("vs", vs_ref, v_ext),
                           ("kw", kw_ref, k_pos_n), ("vw", vw_ref, v_ext)):
        u = seg(name)
        for g in range(NSA_KV_GROUPS):
            ref[0, g] = head(u, g, ext)
    u = seg("zn")
    gn_ref[...] = (u * _sigmoid(u)).astype(BF16)
    sg = _sigmoid(seg("gl"))
    for g in range(NSA_KV_GROUPS):
        gate_ref[0, g] = sg if g == 0 else pltpu.roll(sg, LANES - g * 3 * NSA_HPG, axis=1)
    kc_ref[...] = seg("kc")
    vc_ref[...] = seg("vc")


def _inproj(x2, mod, g_pre, w_packed, B, S):
    M, D = x2.shape
    tm = ROW_TILE
    nt = S // tm
    grid = (M // tm,)
    bs = lambda i: i // nt
    si = lambda i: i % nt
    head_spec = lambda nh: pl.BlockSpec((1, nh, tm, LANES), lambda i: (bs(i), 0, si(i), 0))
    row_spec = lambda w: pl.BlockSpec((tm, w), lambda i: (i, 0))
    q_spec = lambda nh: pl.BlockSpec((1, nh, LANES, tm), lambda i: (bs(i), 0, 0, si(i)))
    sds = jax.ShapeDtypeStruct
    out_shape = (
        sds((B, MOBA_HEADS, LANES, S), BF16),
        sds((B, MOBA_HEADS, S, LANES), BF16),
        sds((B, MOBA_HEADS, S, LANES), BF16),
        sds((B, S // MOBA_BLOCK, 1, 512), F32),
        sds((M, 512), BF16),
        sds((B, NSA_HEADS, LANES, S), BF16),
        sds((B, NSA_KV_GROUPS, S, LANES), BF16),
        sds((B, NSA_KV_GROUPS, S, LANES), BF16),
        sds((B, NSA_KV_GROUPS, S, LANES), BF16),
        sds((B, NSA_KV_GROUPS, S, LANES), BF16),
        sds((M, LANES), F32),
        sds((M, LANES), F32),
        sds((B, NSA_KV_GROUPS, S, LANES), F32),
        sds((M, 512), BF16),
    )
    out_specs = (
        q_spec(MOBA_HEADS), head_spec(MOBA_HEADS), head_spec(MOBA_HEADS),
        pl.BlockSpec((1, tm // MOBA_BLOCK, 1, 512), lambda i: (bs(i), si(i), 0, 0)),
        row_spec(512),
        q_spec(NSA_HEADS),
        head_spec(NSA_KV_GROUPS), head_spec(NSA_KV_GROUPS),
        head_spec(NSA_KV_GROUPS), head_spec(NSA_KV_GROUPS),
        row_spec(LANES), row_spec(LANES),
        head_spec(NSA_KV_GROUPS),
        row_spec(512),
    )
    return pl.pallas_call(
        functools.partial(_inproj_kernel, seq_len=S),
        grid=grid,
        in_specs=[
            row_spec(D),
            pl.BlockSpec((1, 1, 3 * D), lambda i: (bs(i), 0, 0)),
            pl.BlockSpec((1, D), lambda i: (0, 0)),
            pl.BlockSpec((D, D_IN_PACKED), lambda i: (0, 0)),
            pl.BlockSpec((MOBA_HEADS + NSA_HEADS, LANES), lambda i: (0, 0)),
        ],
        out_specs=out_specs,
        out_shape=out_shape,
        compiler_params=_cparams(1),
        name="in_proj",
    )(x2, mod, g_pre, w_packed, _alibi_coef())


def _gelu_tanh(x):
    return 0.5 * x * (1.0 + jnp.tanh(np.sqrt(2.0 / np.pi) * (x + 0.044715 * (x * x * x))))


def _compress_kernel(kc_ref, vc_ref, pek_ref, pev_ref, wk1_ref, wk2_ref, wv1_ref, wv2_ref,
                     m_ref, kout_ref, vout_ref, win_ref):
    S = kc_ref.shape[0]
    nrow = S // CMP_STRIDE
    lane = lax.broadcasted_iota(jnp.int32, (nrow, LANES), 1)
    row = lax.broadcasted_iota(jnp.int32, (nrow, LANES), 0)
    low = lane < HEAD_DIM
    valid = row < nrow - 1
    pos = lane - NSA_POS0
    k_ext = jnp.where((pos >= 0) & (pos < N_POS - 2),
                      jnp.where(pos % 2 == 0, (row + 1).astype(F32), float(POS_RADIX - 1)),
                      jnp.where((pos == N_POS - 2) | (pos == N_POS - 1), 1.0, 0.0))

    def run(src_ref, pe_ref, w1_ref, w2_ref, out_ref, ext, transposed):
        for l in range(CMP_STRIDE):
            win_ref[:, l * LANES:(l + 1) * LANES] = src_ref[pl.ds(l, nrow, stride=CMP_STRIDE), :]
        xw = win_ref[...]
        lo = _dot((xw + pe_ref[0:1, :]).astype(BF16), w1_ref[0])
        hi = _dot((xw + pe_ref[1:2, :]).astype(BF16), w1_ref[1])
        pre = lo + pltpu.roll(hi, nrow - 1, axis=0)
        out = _dot(_gelu_tanh(pre).astype(BF16), w2_ref[...])
        out = jnp.where(valid, out, 0.0)
        for g in range(NSA_KV_GROUPS):
            p = out if g == 0 else pltpu.roll(out, HEAD_DIM, axis=1)
            p = jnp.where(low, p, ext)
            out_ref[0, g] = (p.T if transposed else p).astype(BF16)

    run(kc_ref, pek_ref, wk1_ref, wk2_ref, kout_ref, k_ext, False)
    run(vc_ref, pev_ref, wv1_ref, wv2_ref, vout_ref, jnp.where(valid, m_ref[...], 0.0), True)


def _pack_compress_weights(pe, w1, w2):
    half = CMP_STRIDE
    eye = jnp.eye(NSA_KV_GROUPS, dtype=F32)
    w1r = w1.reshape(2, half, HEAD_DIM, HEAD_DIM)
    w1bd = jnp.einsum("hlde,gk->hlgdke", w1r, eye).reshape(2, half * LANES, LANES).astype(BF16)
    w2bd = jnp.einsum("de,gk->gdke", w2, eye).reshape(LANES, LANES).astype(BF16)
    pet = jnp.tile(pe.reshape(2, half, 1, HEAD_DIM), (1, 1, NSA_KV_GROUPS, 1)).reshape(2, half * LANES)
    return pet, w1bd, w2bd


def _cmp_to_slc_lanes(n_rows, n_slc):
    i = np.arange(n_rows)[:, None]
    j = np.arange(n_slc)[None, :]
    start = i * CMP_STRIDE
    ov = (start < (j + 1) * SLC_BLOCK) & (start + CMP_LEN > j * SLC_BLOCK)
    m = np.zeros((n_rows, LANES), np.float32)
    m[:, NSA_SEL0:NSA_SEL0 + n_slc] = ov
    return jnp.asarray(m)


def _compress(kc, vc, pk, pv, B, S):
    nrow = S // CMP_STRIDE
    pek, wk1, wk2 = pk
    pev, wv1, wv2 = pv
    m = _cmp_to_slc_lanes(nrow, S // SLC_BLOCK)
    full = lambda a: pl.BlockSpec(a.shape, lambda b: (0,) * a.ndim)
    out_spec = pl.BlockSpec((1, NSA_KV_GROUPS, nrow, LANES), lambda b: (b, 0, 0, 0))
    out_sds = jax.ShapeDtypeStruct((B, NSA_KV_GROUPS, nrow, LANES), BF16)
    return pl.pallas_call(
        _compress_kernel,
        grid=(B,),
        in_specs=[pl.BlockSpec((S, LANES), lambda b: (b, 0)), pl.BlockSpec((S, LANES), lambda b: (b, 0)),
                  full(pek), full(pev), full(wk1), full(wk2), full(wv1), full(wv2), full(m)],
        out_specs=(out_spec, out_spec),
        out_shape=(out_sds, out_sds),
        scratch_shapes=[pltpu.VMEM((nrow, CMP_STRIDE * LANES), F32)],
        compiler_params=_cparams(1),
        name="nsa_compress",
    )(kc, vc, pek, pev, wk1, wk2, wv1, wv2, m)


def _rank_rows(vals, count):
    idx = lax.broadcasted_iota(jnp.int32, vals.shape, 0)
    rank = jnp.zeros(vals.shape, jnp.int32)
    for j in range(count):
        rowj = vals[j:j + 1, :]
        beats = (rowj > vals) | ((rowj == vals) & (j < idx))
        rank = rank + beats.astype(jnp.int32)
    return rank


def _tile_masks():
    kpos = lax.broadcasted_iota(jnp.int32, (TILE, TILE), 0)
    qpos = lax.broadcasted_iota(jnp.int32, (TILE, TILE), 1)
    return kpos <= qpos, kpos > qpos


def _tile_slice(j):
    return pl.ds(pl.multiple_of(j * TILE, TILE), TILE)


HALF = TILE // 2
V_ROWS = 80
QK_LEAD = 7
ACC_LAG = 2


def _attend(items, m_ref, acc_ref, fresh):
    n = len(items)
    m_val, acc_val = {}, {}
    scores, updates = {}, {}
    for t in range(n + QK_LEAD + ACC_LAG):
        if t < n:
            _, qt, k, _, mask = items[t]
            s = _dot(k, qt)
            scores[t] = s if mask is None else jnp.where(mask, s, NEG)
        i = t - QK_LEAD
        if 0 <= i < n:
            slot, _, _, vt, _ = items[i]
            s = scores.pop(i)
            if slot not in m_val and not fresh:
                m_val[slot] = m_ref[slot]
            m = m_val.get(slot)
            c0 = jnp.max(s[:HALF], axis=0, keepdims=True)
            m0 = c0 if m is None else jnp.maximum(m, c0)
            p0 = jnp.exp2(s[:HALF] - m0).astype(BF16)
            m1 = jnp.maximum(m0, jnp.max(s[HALF:], axis=0, keepdims=True))
            p1 = jnp.exp2(s[HALF:] - m1).astype(BF16)
            p0 = p0 * jnp.exp2(m0 - m1).astype(BF16)
            updates[i] = (None if m is None else jnp.exp2(m - m1),
                          _dot(vt, jnp.concatenate([p0, p1], axis=0)))
            m_val[slot] = m1
        i = t - QK_LEAD - ACC_LAG
        if 0 <= i < n:
            slot = items[i][0]
            alpha, pv = updates.pop(i)
            if alpha is None:
                acc_val[slot] = pv
            else:
                if slot not in acc_val:
                    acc_val[slot] = acc_ref[slot]
                acc_val[slot] = alpha * acc_val[slot] + pv
    for slot in m_val:
        m_ref[slot] = m_val[slot]
        acc_ref[slot] = acc_val[slot]


def _tile_items(qts, k_of, vt_of, j, mask=None, slot0=0):
    return [(slot0 + h, qt, k_of(h, j), vt_of(h, j), mask) for h, qt in enumerate(qts)]


PAST_GROUP = 4


def _causal_sweep(m_ref, acc_ref, other_items, qts, k_of, vt_of, qi, causal, slot0=0):
    def tiles(js):
        return sum((_tile_items(qts, k_of, vt_of, j, slot0=slot0) for j in js), [])

    def head_block(r):
        def run():
            _attend(other_items() + _tile_items(qts, k_of, vt_of, qi, causal, slot0)
                    + tiles([qi - r + i for i in range(r)]), m_ref, acc_ref, True)
            return jnp.int32(0)
        return run

    def trip(i, carry):
        _attend(tiles([PAST_GROUP * i + c for c in range(PAST_GROUP)]), m_ref, acc_ref, False)
        return carry

    lax.switch(qi % PAST_GROUP, [head_block(r) for r in range(PAST_GROUP)])
    lax.fori_loop(0, qi // PAST_GROUP, trip, jnp.int32(0))


def _normalize(acc):
    return acc[:HEAD_DIM] * (1.0 / acc[HEAD_DIM:HEAD_DIM + 1])


def _fill_transposed(vt_ref, v_ref, n_tiles):
    for c in range(n_tiles):
        vt_ref[c] = v_ref[c * TILE:(c + 1) * TILE, :].T


def _moba_queries(q_ref, kmean_ref, qi):
    heads = range(q_ref.shape[1])
    blk = lax.broadcasted_iota(jnp.int32, (16, TILE), 0)
    qts = [q_ref[0, h] for h in heads]
    gscs = [jnp.where(blk < qi, _dot(kmean_ref[0, h], qts[h]), NEG) for h in heads]
    qsel = []
    for h in heads:
        keep = ((blk < qi) & (_rank_rows(gscs[h], 8) < jnp.minimum(MOBA_TOPK, qi))) | (blk == qi)
        ext = qts[h][MOBA_SEL0:MOBA_SEL0 + 16].astype(F32)
        ext = jnp.where((blk < 8) & jnp.logical_not(keep), NEG, ext).astype(BF16)
        qsel.append(jnp.concatenate([qts[h][:MOBA_SEL0], ext, qts[h][MOBA_SEL0 + 16:]], axis=0))
    return qsel


def _attn_kernel(qm_ref, km_ref, vm_ref, kmean_ref, gm_ref,
                 q_ref, ks_ref, vs_ref, kw_ref, vw_ref, kc_ref, vct_ref, bg_ref, gate_ref,
                 w_ref, x_ref, mod_ref, g_ref,
                 x_out_ref, vmt_ref, vst_ref, vwt_ref, m_ref, acc_ref):
    qi = pl.program_id(1)
    n_tiles = ks_ref.shape[2] // TILE
    n_groups = ks_ref.shape[1]
    groups = range(n_groups)
    heads = range(n_groups * NSA_HPG)
    grp = lambda h: h // NSA_HPG
    n_moba = qm_ref.shape[1]

    @pl.when(qi == 0)
    def _():
        for h in range(n_moba):
            _fill_transposed(vmt_ref.at[h], vm_ref.at[0, h], n_tiles)
        for g in groups:
            _fill_transposed(vst_ref.at[g], vs_ref.at[0, g], n_tiles)
            _fill_transposed(vwt_ref.at[g], vw_ref.at[0, g], n_tiles)

    causal, anti = _tile_masks()
    qsel_m = _moba_queries(qm_ref, kmean_ref, qi)
    qts = [q_ref[0, h] for h in heads]

    n_cmp = kc_ref.shape[2]
    t_c = qi * TILE + lax.broadcasted_iota(jnp.int32, (n_cmp, TILE), 1)
    ok_c = t_c >= lax.broadcasted_iota(jnp.int32, (n_cmp, TILE), 0) * CMP_STRIDE + (CMP_LEN - 1)
    s_cmp = [jnp.where(ok_c, _dot(kc_ref[0, grp(h)], qts[h]), NEG) for h in heads]
    r_cmp = []
    for h in heads:
        e = jnp.exp2(s_cmp[h] - jnp.max(s_cmp[h], axis=0, keepdims=True))
        p = jnp.where(ok_c, e * (1.0 / jnp.sum(e, axis=0, keepdims=True)), 0.0)
        r_cmp.append(_dot(vct_ref[0, grp(h), :NSA_POS0, :], p.astype(BF16)))

    n_sel = NSA_POS0 - NSA_SEL0
    n = lax.broadcasted_iota(jnp.int32, (n_sel, TILE), 0)
    blk = (qi * TILE + lax.broadcasted_iota(jnp.int32, (n_sel, TILE), 1)) // SLC_BLOCK
    forced = (n == 0) | (n == blk) | (n == blk - 1)
    qsel = []
    for g in groups:
        r = [r_cmp[g * NSA_HPG + j][NSA_SEL0:NSA_POS0] for j in range(NSA_HPG)]
        imp = jnp.where(forced, FORCE_BONUS, (r[0] + r[1]) + (r[2] + r[3]))
        imp = jnp.where(n <= blk, imp, NEG)
        bias = lax.cond(
            (qi + 1) * TILE > SLC_TOPN * SLC_BLOCK,
            lambda v: jnp.where((_rank_rows(v, n_sel) < SLC_TOPN) & (n <= blk), 0.0, NEG),
            lambda v: jnp.where(n <= blk, 0.0, NEG), imp).astype(BF16)
        qsel += [jnp.concatenate([qts[g * NSA_HPG + j][:NSA_SEL0], bias, qts[g * NSA_HPG + j][NSA_POS0:]],
                                 axis=0) for j in range(NSA_HPG)]

    wk = lambda h, j: kw_ref[0, grp(h), _tile_slice(jnp.maximum(j, 0)), :]
    wv = lambda h, j: vwt_ref[grp(h), jnp.maximum(j, 0), :V_ROWS, :]
    nh = len(heads)
    window_items = lambda: (_tile_items(qts, wk, wv, qi, causal) + _tile_items(qts, wk, wv, qi - 1, qi >= 1)
                            + _tile_items(qts, wk, wv, qi - 2, jnp.logical_and(anti, qi >= 2)))

    def sweep_k(h, j):
        return km_ref[0, h, _tile_slice(j), :] if h < n_moba else ks_ref[0, grp(h - n_moba), _tile_slice(j), :]

    def sweep_v(h, j):
        return vmt_ref[h, j, :V_ROWS, :] if h < n_moba else vst_ref[grp(h - n_moba), j, :V_ROWS, :]

    _causal_sweep(m_ref, acc_ref, window_items, qsel_m + qsel, sweep_k, sweep_v, qi, causal, slot0=nh)

    o = jnp.concatenate([_normalize(acc_ref[nh + h]) for h in range(n_moba)], axis=0)
    y_m = (o.T * gm_ref[...].astype(F32)).astype(BF16)

    bgs = [bg_ref[0, g].T for g in groups]
    outs = []
    for h in heads:
        bg, r0 = bgs[grp(h)], 3 * (h % NSA_HPG)
        outs.append(bg[r0:r0 + 1] * r_cmp[h][:HEAD_DIM]
                    + bg[r0 + 1:r0 + 2] * _normalize(acc_ref[nh + n_moba + h])
                    + bg[r0 + 2:r0 + 3] * _normalize(acc_ref[h]))
    o = jnp.concatenate(outs, axis=0)
    y_n = (o.T * gate_ref[...].astype(F32)).astype(BF16)

    D = x_ref.shape[1]
    z = _dot(y_m, w_ref[:y_m.shape[1], :]) + _dot(y_n, w_ref[y_m.shape[1]:, :])
    r = z * lax.rsqrt(jnp.mean(z * z, axis=-1, keepdims=True) + RMS_EPS) * g_ref[...]
    x_out_ref[...] = x_ref[...] + mod_ref[0][:, 2 * D:] * r


def _attention(qm, km, vm, kmean, gm, qn, ks, vs, kw, vw, kcmp, vcmp_t, gates, gn,
               w_out, x2, mod, g_post, B, S):
    nq = S // TILE
    D = x2.shape[1]
    nrow = S // CMP_STRIDE
    G = NSA_KV_GROUPS
    per_b = lambda shape: pl.BlockSpec((1,) + shape, lambda b, i: (b, 0, 0, 0))
    per_tile = lambda nh: pl.BlockSpec((1, nh, TILE, LANES), lambda b, i: (b, 0, i, 0))
    q_tile = lambda nh: pl.BlockSpec((1, nh, LANES, TILE), lambda b, i: (b, 0, 0, i))
    rows = lambda w: pl.BlockSpec((TILE, w), lambda b, i: (b * nq + i, 0))
    wide_m, wide_n = MOBA_HEADS * HEAD_DIM, NSA_HEADS * HEAD_DIM
    n_slots = 2 * NSA_HEADS + MOBA_HEADS
    return pl.pallas_call(
        _attn_kernel,
        grid=(B, nq),
        in_specs=[
            q_tile(MOBA_HEADS), per_b((MOBA_HEADS, S, LANES)), per_b((MOBA_HEADS, S, LANES)),
            per_b((MOBA_HEADS, 16, LANES)), rows(wide_m),
            q_tile(NSA_HEADS), per_b((G, S, LANES)), per_b((G, S, LANES)), per_b((G, S, LANES)),
            per_b((G, S, LANES)), per_b((G, nrow, LANES)), per_b((G, nrow, LANES)),
            per_tile(G), rows(wide_n),
            pl.BlockSpec(w_out.shape, lambda b, i: (0, 0)), rows(D),
            pl.BlockSpec((1, 1, 3 * D), lambda b, i: (b, 0, 0)), pl.BlockSpec((1, D), lambda b, i: (0, 0)),
        ],
        out_specs=rows(D),
        out_shape=jax.ShapeDtypeStruct((B * S, D), F32),
        scratch_shapes=[pltpu.VMEM((MOBA_HEADS, nq, LANES, TILE), BF16),
                        pltpu.VMEM((G, nq, LANES, TILE), BF16), pltpu.VMEM((G, nq, LANES, TILE), BF16),
                        pltpu.VMEM((n_slots, 1, TILE), F32), pltpu.VMEM((n_slots, V_ROWS, TILE), F32)],
        compiler_params=_cparams(2),
        name="attention",
    )(qm, km, vm, kmean, gm, qn, ks, vs, kw, vw, kcmp, vcmp_t, gates, gn, w_out, x2, mod, g_post)


def kernel(x, c, w_ada, b_ada, g_pre, g_post, w_in, w_out, pe_k, pe_v, w_ck1, w_ck2, w_cv1, w_cv2):
    B, S, D = x.shape
    L = w_ada.shape[0]
    assert D == (MOBA_HEADS + NSA_HEADS) * HEAD_DIM and S % ROW_TILE == 0
    assert S // MOBA_BLOCK <= 8 and S // SLC_BLOCK <= 32 and S // CMP_STRIDE <= LANES
    nb = S // MOBA_BLOCK

    mod_all = _modulation(c, w_ada, b_ada)
    x2 = x.reshape(B * S, D)
    for l in range(L):
        mod = mod_all[l].reshape(B, 1, 3 * D)
        (qm, km, vm, kmean, gm, qn, ks, vs, kw, vw, kc, vc, gates, gn) = _inproj(
            x2, mod, g_pre[l].reshape(1, D), _pack_w_in(w_in[l]), B, S)
        kmean = kmean.reshape(B, nb, MOBA_HEADS, HEAD_DIM).transpose(0, 2, 1, 3)
        kmean = jnp.pad(kmean, ((0, 0), (0, 0), (0, 16 - nb), (0, LANES - HEAD_DIM))).astype(BF16)
        kcmp, vcmp = _compress(kc, vc, _pack_compress_weights(pe_k[l], w_ck1[l], w_ck2[l]),
                               _pack_compress_weights(pe_v[l], w_cv1[l], w_cv2[l]), B, S)
        x2 = _attention(qm, km, vm, kmean, gm, qn, ks, vs, kw, vw, kcmp, vcmp, gates, gn,
                        w_out[l].astype(BF16), x2, mod, g_post[l].reshape(1, D), B, S)
    return x2.reshape(B, S, D)
```

```python
import functools

import numpy as np
import jax
import jax.numpy as jnp
from jax import lax
from jax.experimental import pallas as pl
from jax.experimental.pallas import tpu as pltpu

HEAD_DIM = 64
MOBA_HEADS = 8
NSA_HEADS = 8
NSA_KV_GROUPS = 2
NSA_HPG = NSA_HEADS // NSA_KV_GROUPS
MOBA_BLOCK = 256
MOBA_TOPK = 3
CMP_LEN = 32
CMP_STRIDE = 16
SLC_BLOCK = 64
SLC_TOPN = 16
WINDOW = 512
RMS_EPS = 1e-6
NEG = -1e9
FORCE_BONUS = 1e4

LANES = 128
TILE = 256
POS_RADIX = 16
ROW_TILE = 512
VMEM_LIMIT = 56 * 1024 * 1024

MOBA_SEL0 = HEAD_DIM
MOBA_POS0 = MOBA_SEL0 + 8
NSA_SEL0 = HEAD_DIM
NSA_POS0 = NSA_SEL0 + 32
N_POS = 8
LOG2E = float(np.log2(np.e))

BF16 = jnp.bfloat16
F32 = jnp.float32


def _slope(i, n):
    return 2.0 ** (-8.0 * (i + 1) / n)


def _alibi_coef():
    table = np.zeros((MOBA_HEADS + NSA_HEADS, LANES), np.float32)
    for row in range(MOBA_HEADS + NSA_HEADS):
        moba = row < MOBA_HEADS
        c = (_slope(row, MOBA_HEADS) if moba else _slope(row - MOBA_HEADS, NSA_HEADS)) * LOG2E
        parts, rest = [], c
        for _ in range(3):
            part = float(np.float32(rest).astype(BF16).astype(np.float32))
            parts.append(part)
            rest -= part
        p0 = MOBA_POS0 if moba else NSA_POS0
        table[row, p0:p0 + N_POS] = [parts[0], parts[0], parts[1], parts[1], parts[2], parts[2], c, c]
    return jnp.asarray(table)


def _dot(a, b):
    return jnp.dot(a, b, preferred_element_type=F32)


def _sigmoid(x):
    return 1.0 / (1.0 + jnp.exp(-x))


def _cparams(n_axes):
    return pltpu.CompilerParams(
        dimension_semantics=("arbitrary",) * n_axes, vmem_limit_bytes=VMEM_LIMIT)


def _mod_kernel(c_ref, w_ref, b_ref, o_ref):
    c = c_ref[...]
    cs = (c * _sigmoid(c)).astype(BF16)
    o_ref[0] = _dot(cs, w_ref[0].astype(BF16)) + b_ref[0]


def _modulation(c, w_ada, b_ada):
    L, D, N = w_ada.shape
    B = c.shape[0]
    tn = 512
    return pl.pallas_call(
        _mod_kernel,
        grid=(L, N // tn),
        in_specs=[
            pl.BlockSpec((B, D), lambda l, j: (0, 0)),
            pl.BlockSpec((1, D, tn), lambda l, j: (l, 0, j)),
            pl.BlockSpec((1, 1, tn), lambda l, j: (l, 0, j)),
        ],
        out_specs=pl.BlockSpec((1, B, tn), lambda l, j: (l, 0, j)),
        out_shape=jax.ShapeDtypeStruct((L, B, N), F32),
        compiler_params=_cparams(2),
        name="adaln_mod",
    )(c, w_ada, b_ada.reshape(L, 1, N))


_SEG = {}
_off = 0
for _name, _w in (("qm", 512), ("km", 512), ("vm", 512), ("zm", 512), ("qn", 512),
                  ("kc", 128), ("vc", 128), ("ks", 128), ("vs", 128), ("kw", 128),
                  ("vw", 128), ("zn", 512), ("gl", 128)):
    _SEG[_name] = (_off, _w)
    _off += _w
D_IN_PACKED = _off


def _pack_w_in(w_in):
    offs = np.cumsum([0, 512, 512, 512, 512, 512, 128, 128, 128, 128, 128, 128, 24, 512])
    parts = [w_in[:, offs[i]:offs[i + 1]] for i in range(13)]
    gl, zn = parts[11], parts[12]
    gl = jnp.pad(gl, ((0, 0), (0, LANES - gl.shape[1])))
    return jnp.concatenate(parts[:11] + [zn, gl], axis=1).astype(BF16)


def _inproj_kernel(x_ref, mod_ref, g_ref, w_ref, coef_ref,
                   qm_ref, km_ref, vm_ref, kmean_ref, gm_ref,
                   qn_ref, ks_ref, vs_ref, kw_ref, vw_ref,
                   kc_ref, vc_ref, gate_ref, gn_ref, *, seq_len):
    tm, D = x_ref.shape
    x = x_ref[...]
    mod = mod_ref[0]
    shift, scale = mod[:, :D], mod[:, D:2 * D]
    y = x * lax.rsqrt(jnp.mean(x * x, axis=-1, keepdims=True) + RMS_EPS) * g_ref[...]
    hb = (y * (1.0 + scale) + shift).astype(BF16)

    def seg(name):
        o, w = _SEG[name]
        return _dot(hb, w_ref[:, o:o + w])

    lane = lax.broadcasted_iota(jnp.int32, (tm, LANES), 1)
    row = lax.broadcasted_iota(jnp.int32, (tm, LANES), 0)
    t = (pl.program_id(0) % (seq_len // tm)) * tm + row
    t_hi = (t // POS_RADIX).astype(F32)
    t_lo = (t % POS_RADIX).astype(F32)

    def pos_lanes(p0, vals):
        out = jnp.zeros((tm, LANES), F32)
        for i, v in enumerate(vals):
            out = jnp.where(lane == p0 + i, v, out)
        return out

    q_pat = [float(POS_RADIX), 1.0] * 3 + [-POS_RADIX * t_hi, -t_lo]
    k_pat = [t_hi, t_lo] * 3 + [1.0, 1.0]
    q_pos_m, k_pos_m = pos_lanes(MOBA_POS0, q_pat), pos_lanes(MOBA_POS0, k_pat)
    q_pos_n, k_pos_n = pos_lanes(NSA_POS0, q_pat), pos_lanes(NSA_POS0, k_pat)
    k_ext_m = k_pos_m + jnp.where(lane - MOBA_SEL0 == t // MOBA_BLOCK, 1.0, 0.0)
    k_ext_s = k_pos_n + jnp.where(lane - NSA_SEL0 == t // SLC_BLOCK, 1.0, 0.0)
    v_ext = jnp.where(lane == HEAD_DIM, 1.0, 0.0)
    low = lane < HEAD_DIM

    def head(u, h, ext):
        p = u[:, (h // 2) * LANES:(h // 2 + 1) * LANES]
        if h % 2:
            p = pltpu.roll(p, HEAD_DIM, axis=1)
        return jnp.where(low, p, ext).astype(BF16)

    u = seg("qm") * (HEAD_DIM ** -0.5 * LOG2E)
    for h in range(MOBA_HEADS):
        qm_ref[0, h] = head(u, h, coef_ref[h:h + 1, :] * q_pos_m).T
    u = seg("km")
    for h in range(MOBA_HEADS):
        km_ref[0, h] = head(u, h, k_ext_m)
    for r in range(tm // MOBA_BLOCK):
        kmean_ref[0, r] = jnp.mean(u[r * MOBA_BLOCK:(r + 1) * MOBA_BLOCK], axis=0, keepdims=True)
    def put_values(ref, h, rows):
        for c in range(tm // TILE):
            ref[0, h, c] = rows[c * TILE:(c + 1) * TILE].T

    u = seg("vm")
    for h in range(MOBA_HEADS):
        put_values(vm_ref, h, head(u, h, v_ext))
    u = seg("zm")
    gm_ref[...] = (u * _sigmoid(u)).astype(BF16)
    u = seg("qn") * (HEAD_DIM ** -0.5 * LOG2E)
    for h in range(NSA_HEADS):
        qn_ref[0, h] = head(u, h, coef_ref[MOBA_HEADS + h:MOBA_HEADS + h + 1, :] * q_pos_n).T
    for name, ref, ext in (("ks", ks_ref, k_ext_s), ("vs", vs_ref, None),
                           ("kw", kw_ref, k_pos_n), ("vw", vw_ref, None)):
        u = seg(name)
        for g in range(NSA_KV_GROUPS):
            if ext is None:
                put_values(ref, g, head(u, g, v_ext))
            else:
                ref[0, g] = head(u, g, ext)
    u = seg("zn")
    gn_ref[...] = (u * _sigmoid(u)).astype(BF16)
    sg = _sigmoid(seg("gl"))
    for g in range(NSA_KV_GROUPS):
        gate_ref[0, g] = sg if g == 0 else pltpu.roll(sg, LANES - g * 3 * NSA_HPG, axis=1)
    kc_ref[...] = seg("kc")
    vc_ref[...] = seg("vc")


def _inproj(x2, mod, g_pre, w_packed, B, S):
    M, D = x2.shape
    tm = ROW_TILE
    nt = S // tm
    grid = (M // tm,)
    bs = lambda i: i // nt
    si = lambda i: i % nt
    head_spec = lambda nh: pl.BlockSpec((1, nh, tm, LANES), lambda i: (bs(i), 0, si(i), 0))
    row_spec = lambda w: pl.BlockSpec((tm, w), lambda i: (i, 0))
    q_spec = lambda nh: pl.BlockSpec((1, nh, LANES, tm), lambda i: (bs(i), 0, 0, si(i)))
    v_spec = lambda nh: pl.BlockSpec((1, nh, tm // TILE, LANES, TILE), lambda i: (bs(i), 0, si(i), 0, 0))
    sds = jax.ShapeDtypeStruct
    out_shape = (
        sds((B, MOBA_HEADS, LANES, S), BF16),
        sds((B, MOBA_HEADS, S, LANES), BF16),
        sds((B, MOBA_HEADS, S // TILE, LANES, TILE), BF16),
        sds((B, S // MOBA_BLOCK, 1, 512), F32),
        sds((M, 512), BF16),
        sds((B, NSA_HEADS, LANES, S), BF16),
        sds((B, NSA_KV_GROUPS, S, LANES), BF16),
        sds((B, NSA_KV_GROUPS, S // TILE, LANES, TILE), BF16),
        sds((B, NSA_KV_GROUPS, S, LANES), BF16),
        sds((B, NSA_KV_GROUPS, S // TILE, LANES, TILE), BF16),
        sds((M, LANES), F32),
        sds((M, LANES), F32),
        sds((B, NSA_KV_GROUPS, S, LANES), F32),
        sds((M, 512), BF16),
    )
    out_specs = (
        q_spec(MOBA_HEADS), head_spec(MOBA_HEADS), v_spec(MOBA_HEADS),
        pl.BlockSpec((1, tm // MOBA_BLOCK, 1, 512), lambda i: (bs(i), si(i), 0, 0)),
        row_spec(512),
        q_spec(NSA_HEADS),
        head_spec(NSA_KV_GROUPS), v_spec(NSA_KV_GROUPS),
        head_spec(NSA_KV_GROUPS), v_spec(NSA_KV_GROUPS),
        row_spec(LANES), row_spec(LANES),
        head_spec(NSA_KV_GROUPS),
        row_spec(512),
    )
    return pl.pallas_call(
        functools.partial(_inproj_kernel, seq_len=S),
        grid=grid,
        in_specs=[
            row_spec(D),
            pl.BlockSpec((1, 1, 3 * D), lambda i: (bs(i), 0, 0)),
            pl.BlockSpec((1, D), lambda i: (0, 0)),
            pl.BlockSpec((D, D_IN_PACKED), lambda i: (0, 0)),
            pl.BlockSpec((MOBA_HEADS + NSA_HEADS, LANES), lambda i: (0, 0)),
        ],
        out_specs=out_specs,
        out_shape=out_shape,
        compiler_params=_cparams(1),
        name="in_proj",
    )(x2, mod, g_pre, w_packed, _alibi_coef())


def _gelu_tanh(x):
    return 0.5 * x * (1.0 + jnp.tanh(np.sqrt(2.0 / np.pi) * (x + 0.044715 * (x * x * x))))


def _compress_kernel(kc_ref, vc_ref, pek_ref, pev_ref, wk1_ref, wk2_ref, wv1_ref, wv2_ref,
                     m_ref, kout_ref, vout_ref, win_ref):
    S = kc_ref.shape[0]
    nrow = S // CMP_STRIDE
    lane = lax.broadcasted_iota(jnp.int32, (nrow, LANES), 1)
    row = lax.broadcasted_iota(jnp.int32, (nrow, LANES), 0)
    low = lane < HEAD_DIM
    valid = row < nrow - 1
    pos = lane - NSA_POS0
    k_ext = jnp.where((pos >= 0) & (pos < N_POS - 2),
                      jnp.where(pos % 2 == 0, (row + 1).astype(F32), float(POS_RADIX - 1)),
                      jnp.where((pos == N_POS - 2) | (pos == N_POS - 1), 1.0, 0.0))

    def run(src_ref, pe_ref, w1_ref, w2_ref, out_ref, ext, transposed):
        for l in range(CMP_STRIDE):
            win_ref[:, l * LANES:(l + 1) * LANES] = src_ref[pl.ds(l, nrow, stride=CMP_STRIDE), :]
        xw = win_ref[...]
        lo = _dot((xw + pe_ref[0:1, :]).astype(BF16), w1_ref[0])
        hi = _dot((xw + pe_ref[1:2, :]).astype(BF16), w1_ref[1])
        pre = lo + pltpu.roll(hi, nrow - 1, axis=0)
        out = _dot(_gelu_tanh(pre).astype(BF16), w2_ref[...])
        out = jnp.where(valid, out, 0.0)
        for g in range(NSA_KV_GROUPS):
            p = out if g == 0 else pltpu.roll(out, HEAD_DIM, axis=1)
            p = jnp.where(low, p, ext)
            out_ref[0, g] = (p.T if transposed else p).astype(BF16)

    run(kc_ref, pek_ref, wk1_ref, wk2_ref, kout_ref, k_ext, False)
    run(vc_ref, pev_ref, wv1_ref, wv2_ref, vout_ref, jnp.where(valid, m_ref[...], 0.0), True)


def _pack_compress_weights(pe, w1, w2):
    half = CMP_STRIDE
    eye = jnp.eye(NSA_KV_GROUPS, dtype=F32)
    w1r = w1.reshape(2, half, HEAD_DIM, HEAD_DIM)
    w1bd = jnp.einsum("hlde,gk->hlgdke", w1r, eye).reshape(2, half * LANES, LANES).astype(BF16)
    w2bd = jnp.einsum("de,gk->gdke", w2, eye).reshape(LANES, LANES).astype(BF16)
    pet = jnp.tile(pe.reshape(2, half, 1, HEAD_DIM), (1, 1, NSA_KV_GROUPS, 1)).reshape(2, half * LANES)
    return pet, w1bd, w2bd


def _cmp_to_slc_lanes(n_rows, n_slc):
    i = np.arange(n_rows)[:, None]
    j = np.arange(n_slc)[None, :]
    start = i * CMP_STRIDE
    ov = (start < (j + 1) * SLC_BLOCK) & (start + CMP_LEN > j * SLC_BLOCK)
    m = np.zeros((n_rows, LANES), np.float32)
    m[:, NSA_SEL0:NSA_SEL0 + n_slc] = ov
    return jnp.asarray(m)


def _compress(kc, vc, pk, pv, B, S):
    nrow = S // CMP_STRIDE
    pek, wk1, wk2 = pk
    pev, wv1, wv2 = pv
    m = _cmp_to_slc_lanes(nrow, S // SLC_BLOCK)
    full = lambda a: pl.BlockSpec(a.shape, lambda b: (0,) * a.ndim)
    out_spec = pl.BlockSpec((1, NSA_KV_GROUPS, nrow, LANES), lambda b: (b, 0, 0, 0))
    out_sds = jax.ShapeDtypeStruct((B, NSA_KV_GROUPS, nrow, LANES), BF16)
    return pl.pallas_call(
        _compress_kernel,
        grid=(B,),
        in_specs=[pl.BlockSpec((S, LANES), lambda b: (b, 0)), pl.BlockSpec((S, LANES), lambda b: (b, 0)),
                  full(pek), full(pev), full(wk1), full(wk2), full(wv1), full(wv2), full(m)],
        out_specs=(out_spec, out_spec),
        out_shape=(out_sds, out_sds),
        scratch_shapes=[pltpu.VMEM((nrow, CMP_STRIDE * LANES), F32)],
        compiler_params=_cparams(1),
        name="nsa_compress",
    )(kc, vc, pek, pev, wk1, wk2, wv1, wv2, m)


def _rank_rows(vals, count):
    idx = lax.broadcasted_iota(jnp.int32, vals.shape, 0)
    rank = jnp.zeros(vals.shape, jnp.int32)
    for j in range(count):
        rowj = vals[j:j + 1, :]
        beats = (rowj > vals) | ((rowj == vals) & (j < idx))
        rank = rank + beats.astype(jnp.int32)
    return rank


def _tile_masks():
    kpos = lax.broadcasted_iota(jnp.int32, (TILE, TILE), 0)
    qpos = lax.broadcasted_iota(jnp.int32, (TILE, TILE), 1)
    return kpos <= qpos, kpos > qpos


def _tile_slice(j):
    return pl.ds(pl.multiple_of(j * TILE, TILE), TILE)


HALF = TILE // 2
V_ROWS = 80
QK_LEAD = 7
ACC_LAG = 2


def _attend(items, m_ref, acc_ref, fresh):
    n = len(items)
    m_val, acc_val = {}, {}
    scores, updates = {}, {}
    for t in range(n + QK_LEAD + ACC_LAG):
        if t < n:
            _, qt, k, _, mask = items[t]
            s = _dot(k, qt)
            scores[t] = s if mask is None else jnp.where(mask, s, NEG)
        i = t - QK_LEAD
        if 0 <= i < n:
            slot, _, _, vt, _ = items[i]
            s = scores.pop(i)
            if slot not in m_val and not fresh:
                m_val[slot] = m_ref[slot]
            m = m_val.get(slot)
            c0 = jnp.max(s[:HALF], axis=0, keepdims=True)
            m0 = c0 if m is None else jnp.maximum(m, c0)
            p0 = jnp.exp2(s[:HALF] - m0).astype(BF16)
            m1 = jnp.maximum(m0, jnp.max(s[HALF:], axis=0, keepdims=True))
            p1 = jnp.exp2(s[HALF:] - m1).astype(BF16)
            p0 = p0 * jnp.exp2(m0 - m1).astype(BF16)
            updates[i] = (None if m is None else jnp.exp2(m - m1),
                          _dot(vt, jnp.concatenate([p0, p1], axis=0)))
            m_val[slot] = m1
        i = t - QK_LEAD - ACC_LAG
        if 0 <= i < n:
            slot = items[i][0]
            alpha, pv = updates.pop(i)
            if alpha is None:
                acc_val[slot] = pv
            else:
                if slot not in acc_val:
                    acc_val[slot] = acc_ref[slot]
                acc_val[slot] = alpha * acc_val[slot] + pv
    for slot in m_val:
        m_ref[slot] = m_val[slot]
        acc_ref[slot] = acc_val[slot]


def _tile_items(qts, k_of, vt_of, j, mask=None, slot0=0):
    return [(slot0 + h, qt, k_of(h, j), vt_of(h, j), mask) for h, qt in enumerate(qts)]


PAST_GROUP = 4


def _causal_sweep(m_ref, acc_ref, other_items, qts, k_of, vt_of, qi, causal, slot0=0):
    def tiles(js):
        return sum((_tile_items(qts, k_of, vt_of, j, slot0=slot0) for j in js), [])

    def head_block(r):
        def run():
            _attend(other_items() + _tile_items(qts, k_of, vt_of, qi, causal, slot0)
                    + tiles([qi - r + i for i in range(r)]), m_ref, acc_ref, True)
            return jnp.int32(0)
        return run

    def trip(i, carry):
        _attend(tiles([PAST_GROUP * i + c for c in range(PAST_GROUP)]), m_ref, acc_ref, False)
        return carry

    lax.switch(qi % PAST_GROUP, [head_block(r) for r in range(PAST_GROUP)])
    lax.fori_loop(0, qi // PAST_GROUP, trip, jnp.int32(0))


def _normalize(acc):
    return acc[:HEAD_DIM] * (1.0 / acc[HEAD_DIM:HEAD_DIM + 1])


def _moba_queries(q_ref, kmean_ref, qi):
    heads = range(q_ref.shape[1])
    blk = lax.broadcasted_iota(jnp.int32, (16, TILE), 0)
    qts = [q_ref[0, h] for h in heads]
    gscs = [jnp.where(blk < qi, _dot(kmean_ref[0, h], qts[h]), NEG) for h in heads]
    qsel = []
    for h in heads:
        keep = ((blk < qi) & (_rank_rows(gscs[h], 8) < jnp.minimum(MOBA_TOPK, qi))) | (blk == qi)
        ext = qts[h][MOBA_SEL0:MOBA_SEL0 + 16].astype(F32)
        ext = jnp.where((blk < 8) & jnp.logical_not(keep), NEG, ext).astype(BF16)
        qsel.append(jnp.concatenate([qts[h][:MOBA_SEL0], ext, qts[h][MOBA_SEL0 + 16:]], axis=0))
    return qsel


def _attn_kernel(qm_ref, km_ref, vm_ref, kmean_ref, gm_ref,
                 q_ref, ks_ref, vs_ref, kw_ref, vw_ref, kc_ref, vct_ref, bg_ref, gate_ref,
                 w_ref, x_ref, mod_ref, g_ref,
                 x_out_ref, m_ref, acc_ref):
    qi = pl.program_id(1)
    n_groups = ks_ref.shape[1]
    groups = range(n_groups)
    heads = range(n_groups * NSA_HPG)
    grp = lambda h: h // NSA_HPG
    n_moba = qm_ref.shape[1]

    causal, anti = _tile_masks()
    qsel_m = _moba_queries(qm_ref, kmean_ref, qi)
    qts = [q_ref[0, h] for h in heads]

    n_cmp = kc_ref.shape[2]
    t_c = qi * TILE + lax.broadcasted_iota(jnp.int32, (n_cmp, TILE), 1)
    ok_c = t_c >= lax.broadcasted_iota(jnp.int32, (n_cmp, TILE), 0) * CMP_STRIDE + (CMP_LEN - 1)
    s_cmp = [jnp.where(ok_c, _dot(kc_ref[0, grp(h)], qts[h]), NEG) for h in heads]
    r_cmp = []
    for h in heads:
        e = jnp.exp2(s_cmp[h] - jnp.max(s_cmp[h], axis=0, keepdims=True))
        p = jnp.where(ok_c, e * (1.0 / jnp.sum(e, axis=0, keepdims=True)), 0.0)
        r_cmp.append(_dot(vct_ref[0, grp(h), :NSA_POS0, :], p.astype(BF16)))

    n_sel = NSA_POS0 - NSA_SEL0
    n = lax.broadcasted_iota(jnp.int32, (n_sel, TILE), 0)
    blk = (qi * TILE + lax.broadcasted_iota(jnp.int32, (n_sel, TILE), 1)) // SLC_BLOCK
    forced = (n == 0) | (n == blk) | (n == blk - 1)
    qsel = []
    for g in groups:
        r = [r_cmp[g * NSA_HPG + j][NSA_SEL0:NSA_POS0] for j in range(NSA_HPG)]
        imp = jnp.where(forced, FORCE_BONUS, (r[0] + r[1]) + (r[2] + r[3]))
        imp = jnp.where(n <= blk, imp, NEG)
        bias = lax.cond(
            (qi + 1) * TILE > SLC_TOPN * SLC_BLOCK,
            lambda v: jnp.where((_rank_rows(v, n_sel) < SLC_TOPN) & (n <= blk), 0.0, NEG),
            lambda v: jnp.where(n <= blk, 0.0, NEG), imp).astype(BF16)
        qsel += [jnp.concatenate([qts[g * NSA_HPG + j][:NSA_SEL0], bias, qts[g * NSA_HPG + j][NSA_POS0:]],
                                 axis=0) for j in range(NSA_HPG)]

    wk = lambda h, j: kw_ref[0, grp(h), _tile_slice(jnp.maximum(j, 0)), :]
    wv = lambda h, j: vw_ref[0, grp(h), jnp.maximum(j, 0), :V_ROWS, :]
    nh = len(heads)
    window_items = lambda: (_tile_items(qts, wk, wv, qi, causal) + _tile_items(qts, wk, wv, qi - 1, qi >= 1)
                            + _tile_items(qts, wk, wv, qi - 2, jnp.logical_and(anti, qi >= 2)))

    def sweep_k(h, j):
        return km_ref[0, h, _tile_slice(j), :] if h < n_moba else ks_ref[0, grp(h - n_moba), _tile_slice(j), :]

    def sweep_v(h, j):
        return vm_ref[0, h, j, :V_ROWS, :] if h < n_moba else vs_ref[0, grp(h - n_moba), j, :V_ROWS, :]

    _causal_sweep(m_ref, acc_ref, window_items, qsel_m + qsel, sweep_k, sweep_v, qi, causal, slot0=nh)

    o = jnp.concatenate([_normalize(acc_ref[nh + h]) for h in range(n_moba)], axis=0)
    y_m = (o.T * gm_ref[...].astype(F32)).astype(BF16)

    bgs = [bg_ref[0, g].T for g in groups]
    outs = []
    for h in heads:
        bg, r0 = bgs[grp(h)], 3 * (h % NSA_HPG)
        outs.append(bg[r0:r0 + 1] * r_cmp[h][:HEAD_DIM]
                    + bg[r0 + 1:r0 + 2] * _normalize(acc_ref[nh + n_moba + h])
                    + bg[r0 + 2:r0 + 3] * _normalize(acc_ref[h]))
    o = jnp.concatenate(outs, axis=0)
    y_n = (o.T * gate_ref[...].astype(F32)).astype(BF16)

    D = x_ref.shape[1]
    z = _dot(y_m, w_ref[:y_m.shape[1], :]) + _dot(y_n, w_ref[y_m.shape[1]:, :])
    r = z * lax.rsqrt(jnp.mean(z * z, axis=-1, keepdims=True) + RMS_EPS) * g_ref[...]
    x_out_ref[...] = x_ref[...] + mod_ref[0][:, 2 * D:] * r


def _attention(qm, km, vm, kmean, gm, qn, ks, vs, kw, vw, kcmp, vcmp_t, gates, gn,
               w_out, x2, mod, g_post, B, S):
    nq = S // TILE
    D = x2.shape[1]
    nrow = S // CMP_STRIDE
    G = NSA_KV_GROUPS
    per_b = lambda shape: pl.BlockSpec((1,) + shape, lambda b, i: (b, 0, 0, 0))
    per_tile = lambda nh: pl.BlockSpec((1, nh, TILE, LANES), lambda b, i: (b, 0, i, 0))
    q_tile = lambda nh: pl.BlockSpec((1, nh, LANES, TILE), lambda b, i: (b, 0, 0, i))
    v_all = lambda nh: pl.BlockSpec((1, nh, nq, LANES, TILE), lambda b, i: (b, 0, 0, 0, 0))
    rows = lambda w: pl.BlockSpec((TILE, w), lambda b, i: (b * nq + i, 0))
    wide_m, wide_n = MOBA_HEADS * HEAD_DIM, NSA_HEADS * HEAD_DIM
    n_slots = 2 * NSA_HEADS + MOBA_HEADS
    return pl.pallas_call(
        _attn_kernel,
        grid=(B, nq),
        in_specs=[
            q_tile(MOBA_HEADS), per_b((MOBA_HEADS, S, LANES)), v_all(MOBA_HEADS),
            per_b((MOBA_HEADS, 16, LANES)), rows(wide_m),
            q_tile(NSA_HEADS), per_b((G, S, LANES)), v_all(G), per_b((G, S, LANES)),
            v_all(G), per_b((G, nrow, LANES)), per_b((G, nrow, LANES)),
            per_tile(G), rows(wide_n),
            pl.BlockSpec(w_out.shape, lambda b, i: (0, 0)), rows(D),
            pl.BlockSpec((1, 1, 3 * D), lambda b, i: (b, 0, 0)), pl.BlockSpec((1, D), lambda b, i: (0, 0)),
        ],
        out_specs=rows(D),
        out_shape=jax.ShapeDtypeStruct((B * S, D), F32),
        scratch_shapes=[pltpu.VMEM((n_slots, 1, TILE), F32), pltpu.VMEM((n_slots, V_ROWS, TILE), F32)],
        compiler_params=_cparams(2),
        name="attention",
    )(qm, km, vm, kmean, gm, qn, ks, vs, kw, vw, kcmp, vcmp_t, gates, gn, w_out, x2, mod, g_post)


def kernel(x, c, w_ada, b_ada, g_pre, g_post, w_in, w_out, pe_k, pe_v, w_ck1, w_ck2, w_cv1, w_cv2):
    B, S, D = x.shape
    L = w_ada.shape[0]
    assert D == (MOBA_HEADS + NSA_HEADS) * HEAD_DIM and S % ROW_TILE == 0
    assert S // MOBA_BLOCK <= 8 and S // SLC_BLOCK <= 32 and S // CMP_STRIDE <= LANES
    nb = S // MOBA_BLOCK

    mod_all = _modulation(c, w_ada, b_ada)
    x2 = x.reshape(B * S, D)
    for l in range(L):
        mod = mod_all[l].reshape(B, 1, 3 * D)
        (qm, km, vm, kmean, gm, qn, ks, vs, kw, vw, kc, vc, gates, gn) = _inproj(
            x2, mod, g_pre[l].reshape(1, D), _pack_w_in(w_in[l]), B, S)
        kmean = kmean.reshape(B, nb, MOBA_HEADS, HEAD_DIM).transpose(0, 2, 1, 3)
        kmean = jnp.pad(kmean, ((0, 0), (0, 0), (0, 16 - nb), (0, LANES - HEAD_DIM))).astype(BF16)
        kcmp, vcmp = _compress(kc, vc, _pack_compress_weights(pe_k[l], w_ck1[l], w_ck2[l]),
                               _pack_compress_weights(pe_v[l], w_cv1[l], w_cv2[l]), B, S)
        x2 = _attention(qm, km, vm, kmean, gm, qn, ks, vs, kw, vw, kcmp, vcmp, gates, gn,
                        w_out[l].astype(BF16), x2, mod, g_post[l].reshape(1, D), B, S)
    return x2.reshape(B, S, D)
```
